```python
import math
import jax, jax.numpy as jnp
from jax import lax
import numpy as np

D_MODEL = 1024
BATCH = 8
SEQ = 8192
DEPTH = 1

D_MIX = D_MODEL
D_ATTN = D_MIX // 2
D_SSM = D_MIX - D_ATTN
N_HEADS = 4
QK_NOPE = 128
QK_ROPE = 64
V_HEAD = D_ATTN // N_HEADS
Q_LORA = 384
KV_LORA = 256
ROPE_THETA = 10000.0
Q_BLOCK = 128
MAX_POS_OFFSET = 4096
SSM_GROUP = 16
N_SSM_GROUPS = D_SSM // SSM_GROUP
SSM_STATE = 64
DT_MIN = 1e-3
DT_MAX = 1e-1
SSM_C_STD = 0.5
D_FF = 2816
CONV_W = 3
EPS = 1e-6
IN_COLS = Q_LORA + KV_LORA + QK_ROPE + D_SSM
OFF_KV = Q_LORA
OFF_KR = Q_LORA + KV_LORA
OFF_U = Q_LORA + KV_LORA + QK_ROPE

kernel_name = 'hymba_mla_s5_convglu_adaln'


def _rmsnorm(x, g):
    xf = x.astype(jnp.float32)
    xf = xf * lax.rsqrt(jnp.mean(xf * xf, axis=-1, keepdims=True) + EPS)
    return (xf * g.astype(jnp.float32)).astype(x.dtype)


def _modulate(h, shift, scale):
    return h * (1.0 + scale[:, None, :]) + shift[:, None, :]


def _rope_tables(positions):
    inv_freq = ROPE_THETA ** (-jnp.arange(0, QK_ROPE, 2, dtype=jnp.float32) / QK_ROPE)
    ang = positions.astype(jnp.float32)[..., None] * inv_freq
    return jnp.cos(ang), jnp.sin(ang)


def _apply_rope(x, cos, sin):
    xf = x.astype(jnp.float32)
    x1, x2 = xf[..., :QK_ROPE // 2], xf[..., QK_ROPE // 2:]
    return jnp.concatenate([x1 * cos - x2 * sin, x1 * sin + x2 * cos], axis=-1).astype(x.dtype)


def _mla(zq, zkv, zkr, positions, q_norm_g, w_uq, kv_norm_g, w_ukv):
    bsz, seq, _ = zq.shape
    cos, sin = _rope_tables(positions)
    q = (_rmsnorm(zq, q_norm_g) @ w_uq).reshape(bsz, seq, N_HEADS, QK_NOPE + QK_ROPE)
    q_nope = q[..., :QK_NOPE]
    q_rope = _apply_rope(q[..., QK_NOPE:], cos[:, :, None, :], sin[:, :, None, :])
    kv = (_rmsnorm(zkv, kv_norm_g) @ w_ukv).reshape(bsz, seq, N_HEADS, QK_NOPE + V_HEAD)
    k_nope, v = kv[..., :QK_NOPE], kv[..., QK_NOPE:]
    k_rope = _apply_rope(zkr, cos, sin)
    n_blk = seq // Q_BLOCK
    qn_blk = q_nope.reshape(bsz, n_blk, Q_BLOCK, N_HEADS, QK_NOPE).transpose(1, 0, 2, 3, 4)
    qr_blk = q_rope.reshape(bsz, n_blk, Q_BLOCK, N_HEADS, QK_ROPE).transpose(1, 0, 2, 3, 4)
    key_idx = jnp.arange(seq)
    scale = (QK_NOPE + QK_ROPE) ** -0.5

    def one_block(args):
        qn, qr, blk = args
        s = (jnp.einsum('bqhd,bkhd->bhqk', qn, k_nope)
             + jnp.einsum('bqhr,bkr->bhqk', qr, k_rope)).astype(jnp.float32) * scale
        q_idx = blk * Q_BLOCK + jnp.arange(Q_BLOCK)
        s = jnp.where(key_idx[None, :] <= q_idx[:, None], s, -1e30)
        p = jax.nn.softmax(s, axis=-1).astype(v.dtype)
        return jnp.einsum('bhqk,bkhd->bqhd', p, v)

    o = lax.map(one_block, (qn_blk, qr_blk, jnp.arange(n_blk)))
    return o.transpose(1, 0, 2, 3, 4).reshape(bsz, seq, N_HEADS * V_HEAD)


def _ssm_combine(e1, e2):
    a1r, a1i, b1r, b1i = e1
    a2r, a2i, b2r, b2i = e2
    return (a2r * a1r - a2i * a1i,
            a2r * a1i + a2i * a1r,
            a2r * b1r - a2i * b1i + b2r,
            a2r * b1i + a2i * b1r + b2i)


def _s5(zu, lam_re, lam_im, log_dt, b_re, b_im, c_re, c_im, d_skip, w_glu, b_glu):
    bsz, seq, _ = zu.shape
    f32 = jnp.float32
    uf = zu.astype(f32).reshape(bsz, seq, N_SSM_GROUPS, SSM_GROUP)
    lr = jnp.minimum(lam_re.astype(f32), -1e-4)
    li = lam_im.astype(f32)
    dt = jnp.exp(log_dt.astype(f32))[:, None]
    mag = jnp.exp(lr * dt)
    ab_re = mag * jnp.cos(li * dt)
    ab_im = mag * jnp.sin(li * dt)
    den = lr * lr + li * li
    nr, ni = ab_re - 1.0, ab_im
    z_re = ((nr * lr + ni * li) / den)[..., None]
    z_im = ((ni * lr - nr * li) / den)[..., None]
    br, bi = b_re.astype(f32), b_im.astype(f32)
    bb_re = z_re * br - z_im * bi
    bb_im = z_re * bi + z_im * br
    bu_re = jnp.einsum('bsgh,gph->sbgp', uf, bb_re)
    bu_im = jnp.einsum('bsgh,gph->sbgp', uf, bb_im)
    a_re = jnp.broadcast_to(ab_re[None, None], (seq, 1, N_SSM_GROUPS, SSM_STATE))
    a_im = jnp.broadcast_to(ab_im[None, None], (seq, 1, N_SSM_GROUPS, SSM_STATE))
    _, _, xr, xi = lax.associative_scan(_ssm_combine, (a_re, a_im, bu_re, bu_im), axis=0)
    y = (jnp.einsum('sbgp,ghp->bsgh', xr, c_re.astype(f32))
         - jnp.einsum('sbgp,ghp->bsgh', xi, c_im.astype(f32))
         + d_skip.astype(f32) * uf)
    y = jax.nn.gelu(y.reshape(bsz, seq, D_SSM))
    gl = y @ w_glu.astype(f32) + b_glu.astype(f32)
    out = gl[..., :D_SSM] * jax.nn.sigmoid(gl[..., D_SSM:])
    return out.astype(zu.dtype)


def _causal_dwconv(x, w, b):
    seq = x.shape[1]
    xp = jnp.pad(x, ((0, 0), (CONV_W - 1, 0), (0, 0)))
    y = b
    for k in range(CONV_W):
        y = y + w[k] * xp[:, k:k + seq, :]
    return y


def setup_inputs(seed: int = 0) -> dict:
    key = jax.random.key(seed)
    ks = jax.random.split(key, 32)
    f32 = jnp.float32
    L, G, P, H = DEPTH, N_SSM_GROUPS, SSM_STATE, SSM_GROUP

    def nrm(k, shape, std):
        return jax.random.normal(k, shape, f32) * std

    x = nrm(ks[0], (BATCH, SEQ, D_MODEL), 1.0)
    c = nrm(ks[1], (BATCH, D_MODEL), 1.0)
    offsets = jax.random.randint(ks[2], (BATCH, 1), 0, MAX_POS_OFFSET, dtype=jnp.int32)
    positions = offsets + jnp.arange(SEQ, dtype=jnp.int32)[None, :]
    return {
        'x': x,
        'c': c,
        'positions': positions,
        'w_mod': nrm(ks[3], (L, D_MODEL, 6 * D_MODEL), D_MODEL ** -0.5),
        'b_mod': nrm(ks[4], (L, 6 * D_MODEL), 0.02),
        'ln1_g': 1.0 + nrm(ks[5], (L, D_MODEL), 0.02),
        'w_in': nrm(ks[6], (L, D_MODEL, IN_COLS), D_MODEL ** -0.5),
        'q_norm_g': 1.0 + nrm(ks[7], (L, Q_LORA), 0.02),
        'w_uq': nrm(ks[8], (L, Q_LORA, N_HEADS * (QK_NOPE + QK_ROPE)), Q_LORA ** -0.5),
        'kv_norm_g': 1.0 + nrm(ks[9], (L, KV_LORA), 0.02),
        'w_ukv': nrm(ks[10], (L, KV_LORA, N_HEADS * (QK_NOPE + V_HEAD)), KV_LORA ** -0.5),
        'ssm_lam_re': -0.5 + nrm(ks[11], (L, G, P), 0.01),
        'ssm_lam_im': jnp.tile(math.pi * jnp.arange(P, dtype=f32), (L, G, 1)),
        'ssm_log_dt': jax.random.uniform(ks[12], (L, G), f32, math.log(DT_MIN), math.log(DT_MAX)),
        'ssm_b_re': nrm(ks[13], (L, G, P, H), (2 * H) ** -0.5),
        'ssm_b_im': nrm(ks[14], (L, G, P, H), (2 * H) ** -0.5),
        'ssm_c_re': nrm(ks[15], (L, G, H, P), SSM_C_STD),
        'ssm_c_im': nrm(ks[16], (L, G, H, P), SSM_C_STD),
        'ssm_d': nrm(ks[17], (L, G, H), 1.0),
        'w_glu': nrm(ks[18], (L, D_SSM, 2 * D_SSM), D_SSM ** -0.5),
        'b_glu': nrm(ks[19], (L, 2 * D_SSM), 0.02),
        'attn_out_g': 1.0 + nrm(ks[20], (L, D_ATTN), 0.02),
        'ssm_out_g': 1.0 + nrm(ks[21], (L, D_SSM), 0.02),
        'w_out': nrm(ks[22], (L, D_MIX, D_MODEL), D_MIX ** -0.5),
        'ln2_g': 1.0 + nrm(ks[23], (L, D_MODEL), 0.02),
        'w_up': nrm(ks[24], (L, D_MODEL, 2 * D_FF), D_MODEL ** -0.5),
        'conv_w': nrm(ks[25], (L, CONV_W, D_FF), CONV_W ** -0.5),
        'conv_b': nrm(ks[26], (L, D_FF), 0.02),
        'w_down': nrm(ks[27], (L, D_FF, D_MODEL), D_FF ** -0.5),
        'final_g': 1.0 + nrm(ks[28], (D_MODEL,), 0.02),
    }


def reference(x, c, positions, w_mod, b_mod, ln1_g, w_in, q_norm_g, w_uq, kv_norm_g, w_ukv,
              ssm_lam_re, ssm_lam_im, ssm_log_dt, ssm_b_re, ssm_b_im, ssm_c_re, ssm_c_im,
              ssm_d, w_glu, b_glu, attn_out_g, ssm_out_g, w_out, ln2_g, w_up, conv_w, conv_b,
              w_down, final_g):
    cond = jax.nn.silu(c)
    for l in range(DEPTH):
        mod = cond @ w_mod[l] + b_mod[l]
        sh_a, sc_a, g_a, sh_f, sc_f, g_f = jnp.split(mod, 6, axis=-1)
        h = _modulate(_rmsnorm(x, ln1_g[l]), sh_a, sc_a)
        z = h @ w_in[l]
        a = _mla(z[..., :OFF_KV], z[..., OFF_KV:OFF_KR], z[..., OFF_KR:OFF_U], positions,
                 q_norm_g[l], w_uq[l], kv_norm_g[l], w_ukv[l])
        s = _s5(z[..., OFF_U:], ssm_lam_re[l], ssm_lam_im[l], ssm_log_dt[l], ssm_b_re[l],
                ssm_b_im[l], ssm_c_re[l], ssm_c_im[l], ssm_d[l], w_glu[l], b_glu[l])
        m = jnp.concatenate([_rmsnorm(a, attn_out_g[l]), _rmsnorm(s, ssm_out_g[l])], axis=-1) @ w_out[l]
        x = x + g_a[:, None, :] * m
        h = _modulate(_rmsnorm(x, ln2_g[l]), sh_f, sc_f)
        up = h @ w_up[l]
        gate_in, val = up[..., :D_FF], up[..., D_FF:]
        f = (jax.nn.gelu(_causal_dwconv(gate_in, conv_w[l], conv_b[l])) * val) @ w_down[l]
        x = x + g_f[:, None, :] * f
    return _rmsnorm(x, final_g)
```

```python
import functools
import math

import jax
import jax.numpy as jnp
from jax import lax
from jax.experimental import pallas as pl
from jax.experimental.pallas import tpu as pltpu

N_HEADS = 4
QK_NOPE = 128
QK_ROPE = 64
V_HEAD = 128
Q_LORA = 384
KV_LORA = 256
ROPE_THETA = 10000.0
SSM_GROUP = 16
SSM_STATE = 64
CONV_W = 3
EPS = 1e-6

HEAD_PAD = 256
SSM_CHUNK = 16
FF_CHUNK = 256
HALO = 8
VMEM_LIMIT = 56 * 1024 * 1024

F32 = jnp.float32
BF16 = jnp.bfloat16


def _rms(x, g):
    return x * lax.rsqrt(jnp.mean(x * x, axis=-1, keepdims=True) + EPS) * g


def _gelu(x):
    return 0.5 * x * (1.0 + jnp.tanh(math.sqrt(2.0 / math.pi) * (x + 0.044715 * (x * x * x))))


def _sigmoid(x):
    return 1.0 / (1.0 + jnp.exp(-x))


def _dot(a, b):
    return jnp.dot(a, b, preferred_element_type=F32)


def _params(*sem):
    return pltpu.CompilerParams(dimension_semantics=sem, vmem_limit_bytes=VMEM_LIMIT)


def _mod_kernel(c_ref, w_ref, b_ref, o_ref):
    c = c_ref[...]
    cond = c * _sigmoid(c)
    o_ref[...] = jnp.dot(cond, w_ref[...], preferred_element_type=F32,
                         precision=lax.Precision.HIGHEST) + b_ref[...]


def _mod(c, w_mod, b_mod):
    bsz, d = c.shape
    n = w_mod.shape[1]
    tn = 1024
    return pl.pallas_call(
        _mod_kernel,
        grid=(n // tn,),
        in_specs=[pl.BlockSpec((bsz, d), lambda j: (0, 0)),
                  pl.BlockSpec((d, tn), lambda j: (0, j)),
                  pl.BlockSpec((1, tn), lambda j: (0, j))],
        out_specs=pl.BlockSpec((bsz, tn), lambda j: (0, j)),
        out_shape=jax.ShapeDtypeStruct((bsz, n), F32),
        compiler_params=_params("arbitrary"),
        name="mod",
    )(c, w_mod, b_mod.reshape(1, n))


def _rope_kernel(pos_ref, f_ref, cos_ref, sin_ref):
    ang = pos_ref[...].astype(F32) * f_ref[...]
    cos_ref[...] = jnp.cos(ang)
    sin_ref[...] = jnp.sin(ang)


def _rope_tables(positions):
    bsz, seq = positions.shape
    half = QK_ROPE // 2
    inv_freq = ROPE_THETA ** (-jnp.arange(0, QK_ROPE, 2, dtype=F32) / QK_ROPE)
    rows = bsz * seq * half // 128
    pos_rep = jnp.repeat(positions.reshape(-1), half).reshape(rows, 128)
    f_tile = jnp.tile(inv_freq, 128 // half).reshape(1, 128)
    tr = min(rows, 2048)
    cos, sin = pl.pallas_call(
        _rope_kernel,
        grid=(rows // tr,),
        in_specs=[pl.BlockSpec((tr, 128), lambda i: (i, 0)),
                  pl.BlockSpec((1, 128), lambda i: (0, 0))],
        out_specs=[pl.BlockSpec((tr, 128), lambda i: (i, 0))] * 2,
        out_shape=[jax.ShapeDtypeStruct((rows, 128), F32)] * 2,
        compiler_params=_params("arbitrary"),
        name="rope",
    )(pos_rep, f_tile)
    return cos.reshape(bsz, seq, half), sin.reshape(bsz, seq, half)


def _inproj_kernel(x_ref, mod_ref, ln_ref, win_ref, qg_ref, wuq_ref, kvg_ref, wukv_ref,
                   cc_ref, ss_ref, q_ref, k_ref, v_ref, u_ref, *, scale):
    x = x_ref[0]
    h = _rms(x, ln_ref[...]) * (1.0 + mod_ref[0, 1:2, :]) + mod_ref[0, 0:1, :]
    z = _dot(h.astype(BF16), win_ref[...])
    o_kv = Q_LORA
    o_u = o_kv + KV_LORA
    o_kr = z.shape[1] - 128
    q2 = _dot(_rms(z[:, :o_kv], qg_ref[...]).astype(BF16), wuq_ref[...])
    kv = _dot(_rms(z[:, o_kv:o_u], kvg_ref[...]).astype(BF16), wukv_ref[...])
    u_ref[0] = z[:, o_u:o_kr]
    cc = cc_ref[0]
    ss = ss_ref[0]
    zkr = z[:, o_kr:]
    kr = (zkr * cc + pltpu.roll(zkr, QK_ROPE // 2, 1) * ss).astype(BF16)
    for hd in range(N_HEADS):
        qn = q2[:, HEAD_PAD * hd:HEAD_PAD * hd + QK_NOPE]
        qr = q2[:, HEAD_PAD * hd + QK_NOPE:HEAD_PAD * (hd + 1)]
        qr = qr * cc + pltpu.roll(qr, QK_ROPE // 2, 1) * ss
        q_ref[0, hd, :, :QK_NOPE] = (qn * scale).astype(BF16)
        q_ref[0, hd, :, QK_NOPE:] = (qr * scale).astype(BF16)
        k_ref[0, hd, :, :QK_NOPE] = kv[:, QK_NOPE * hd:QK_NOPE * (hd + 1)].astype(BF16)
        k_ref[0, hd, :, QK_NOPE:] = kr
        v_off = N_HEADS * QK_NOPE + V_HEAD * hd
        v_ref[0, hd] = kv[:, v_off:v_off + V_HEAD].astype(BF16)


def _inproj(x, mod, ln1_g, w_in_r, q_norm_g, w_uq_r, kv_norm_g, w_ukv_r, cc, ss, tm):
    bsz, seq, d = x.shape
    d_ssm = w_in_r.shape[1] - Q_LORA - KV_LORA - 128
    const = lambda shape: pl.BlockSpec(shape, lambda b, i: (0,) * len(shape))
    scale = (QK_NOPE + QK_ROPE) ** -0.5
    return pl.pallas_call(
        functools.partial(_inproj_kernel, scale=scale),
        grid=(bsz, seq // tm),
        in_specs=[pl.BlockSpec((1, tm, d), lambda b, i: (b, i, 0)),
                  pl.BlockSpec((1, 6, d), lambda b, i: (b, 0, 0)),
                  const((1, d)), const(w_in_r.shape), const((1, Q_LORA)), const(w_uq_r.shape),
                  const((1, KV_LORA)), const(w_ukv_r.shape),
                  pl.BlockSpec((1, tm, 128), lambda b, i: (b, i, 0)),
                  pl.BlockSpec((1, tm, 128), lambda b, i: (b, i, 0))],
        out_specs=[pl.BlockSpec((1, N_HEADS, tm, HEAD_PAD), lambda b, i: (b, 0, i, 0)),
                   pl.BlockSpec((1, N_HEADS, tm, HEAD_PAD), lambda b, i: (b, 0, i, 0)),
                   pl.BlockSpec((1, N_HEADS, tm, V_HEAD), lambda b, i: (b, 0, i, 0)),
                   pl.BlockSpec((1, tm, d_ssm), lambda b, i: (b, i, 0))],
        out_shape=[jax.ShapeDtypeStruct((bsz, N_HEADS, seq, HEAD_PAD), BF16),
                   jax.ShapeDtypeStruct((bsz, N_HEADS, seq, HEAD_PAD), BF16),
                   jax.ShapeDtypeStruct((bsz, N_HEADS, seq, V_HEAD), BF16),
                   jax.ShapeDtypeStruct((bsz, seq, d_ssm), F32)],
        compiler_params=_params("arbitrary", "arbitrary"),
        name="inproj",
    )(x, mod, ln1_g.reshape(1, d), w_in_r, q_norm_g.reshape(1, -1), w_uq_r,
      kv_norm_g.reshape(1, -1), w_ukv_r, cc, ss)


def _attn_kernel(q_ref, k_ref, v_ref, o_ref, m_sc, l_sc, acc_sc, *, tq):
    qi = pl.program_id(2)
    q = q_ref[0, 0]
    m_sc[...] = jnp.full(m_sc.shape, -jnp.inf, F32)
    l_sc[...] = jnp.zeros(l_sc.shape, F32)
    acc_sc[...] = jnp.zeros(acc_sc.shape, F32)

    def block(kb, masked):
        start = pl.multiple_of(kb * tq, tq)
        k = k_ref[0, 0, pl.ds(start, tq), :]
        v = v_ref[0, 0, pl.ds(start, tq), :]
        s = lax.dot_general(q, k, (((1,), (1,)), ((), ())), preferred_element_type=F32)
        if masked:
            row = lax.broadcasted_iota(jnp.int32, s.shape, 0)
            col = lax.broadcasted_iota(jnp.int32, s.shape, 1)
            s = jnp.where(col <= row, s, -1e30)
        m_prev = m_sc[...]
        m_new = jnp.maximum(m_prev, jnp.max(s, axis=-1, keepdims=True))
        alpha = jnp.exp(m_prev - m_new)
        p = jnp.exp(s - m_new)
        l_sc[...] = alpha * l_sc[...] + jnp.sum(p, axis=-1, keepdims=True)
        acc_sc[...] = alpha * acc_sc[...] + _dot(p.astype(BF16), v)
        m_sc[...] = m_new

    def body(kb, carry):
        block(kb, False)
        return carry

    lax.fori_loop(0, qi, body, 0)
    block(qi, True)
    o_ref[0] = (acc_sc[...] / l_sc[...]).astype(o_ref.dtype)


def _attention(q, k, v, tq):
    bsz, nh, seq, dh = q.shape
    dv = v.shape[-1]
    return pl.pallas_call(
        functools.partial(_attn_kernel, tq=tq),
        grid=(bsz, nh, seq // tq),
        in_specs=[pl.BlockSpec((1, 1, tq, dh), lambda b, h, i: (b, h, i, 0)),
                  pl.BlockSpec((1, 1, seq, dh), lambda b, h, i: (b, h, 0, 0)),
                  pl.BlockSpec((1, 1, seq, dv), lambda b, h, i: (b, h, 0, 0))],
        out_specs=pl.BlockSpec((1, tq, dv), lambda b, h, i: (b, i, h)),
        out_shape=jax.ShapeDtypeStruct((bsz, seq, nh * dv), BF16),
        scratch_shapes=[pltpu.VMEM((tq, 1), F32), pltpu.VMEM((tq, 1), F32),
                        pltpu.VMEM((tq, dv), F32)],
        compiler_params=_params("arbitrary", "arbitrary", "arbitrary"),
        name="attention",
    )(q, k, v)


def _ssm_kernel(x_ref, toep_ref, rre_ref, rim_ref, ore_ref, oim_ref, are_ref, aim_ref, y_ref,
                r_re, r_im, xp_re, xp_im, *, bsz, n_chunks):
    x = x_ref[0]
    r_re[...] = _dot(x, rre_ref[0])
    r_im[...] = _dot(x, rim_ref[0])
    a_re = jnp.broadcast_to(are_ref[0], (bsz, are_ref.shape[-1]))
    a_im = jnp.broadcast_to(aim_ref[0], (bsz, aim_ref.shape[-1]))

    def step(c, carry):
        s_re, s_im = carry
        rows = pl.ds(pl.multiple_of(c * bsz, bsz), bsz)
        xp_re[rows, :] = s_re
        xp_im[rows, :] = s_im
        n_re = a_re * s_re - a_im * s_im + r_re[rows, :]
        n_im = a_re * s_im + a_im * s_re + r_im[rows, :]
        return n_re, n_im

    zero = jnp.zeros((bsz, r_re.shape[-1]), F32)
    lax.fori_loop(0, n_chunks, step, (zero, zero), unroll=8)
    y_ref[0] = (_dot(x, toep_ref[0]) + _dot(xp_re[...].astype(BF16), ore_ref[0])
                + _dot(xp_im[...].astype(BF16), oim_ref[0]))


def _ssm(xg, toep, r_re, r_im, o_re, o_im, a_re, a_im, bsz):
    npair, rows, width = xg.shape
    ns = r_re.shape[-1]
    blk = lambda a: pl.BlockSpec((1,) + a.shape[1:], lambda g: (g,) + (0,) * (a.ndim - 1))
    return pl.pallas_call(
        functools.partial(_ssm_kernel, bsz=bsz, n_chunks=rows // bsz),
        grid=(npair,),
        in_specs=[blk(xg), blk(toep), blk(r_re), blk(r_im), blk(o_re), blk(o_im), blk(a_re), blk(a_im)],
        out_specs=pl.BlockSpec((1, rows, width), lambda g: (g, 0, 0)),
        out_shape=jax.ShapeDtypeStruct((npair, rows, width), F32),
        scratch_shapes=[pltpu.VMEM((rows, ns), F32)] * 4,
        compiler_params=_params("arbitrary"),
        name="ssm",
    )(xg, toep, r_re, r_im, o_re, o_im, a_re, a_im)


def _ssm_operators(lam_re, lam_im, log_dt, b_re, b_im, c_re, c_im, d_skip):
    t_len = SSM_CHUNK
    g, p = lam_re.shape
    hch = b_re.shape[-1]
    lr = jnp.minimum(lam_re.astype(F32), -1e-4)
    li = lam_im.astype(F32)
    dt = jnp.exp(log_dt.astype(F32))[:, None]
    mag = jnp.exp(lr * dt)
    ab_re = mag * jnp.cos(li * dt)
    ab_im = mag * jnp.sin(li * dt)
    den = lr * lr + li * li
    nr, ni = ab_re - 1.0, ab_im
    z_re = ((nr * lr + ni * li) / den)[..., None]
    z_im = ((ni * lr - nr * li) / den)[..., None]
    br, bi = b_re.astype(F32), b_im.astype(F32)
    bb_re = z_re * br - z_im * bi
    bb_im = z_re * bi + z_im * br
    tau = jnp.arange(t_len + 1, dtype=F32)[:, None, None]
    pm = jnp.exp(tau * (lr * dt)[None])
    pw_re = pm * jnp.cos(tau * (li * dt)[None])
    pw_im = pm * jnp.sin(tau * (li * dt)[None])
    cr, ci = c_re.astype(F32), c_im.astype(F32)
    hi = lax.Precision.HIGHEST
    ca_re = cr[None] * pw_re[:, :, None, :] - ci[None] * pw_im[:, :, None, :]
    ca_im = cr[None] * pw_im[:, :, None, :] + ci[None] * pw_re[:, :, None, :]
    kern = (jnp.einsum('tgop,gph->tgoh', ca_re[:t_len], bb_re, precision=hi)
            - jnp.einsum('tgop,gph->tgoh', ca_im[:t_len], bb_im, precision=hi))
    kern = kern.at[0].add(jnp.eye(hch, dtype=F32)[None] * d_skip.astype(F32)[:, :, None])
    lag = jnp.arange(t_len)[:, None] - jnp.arange(t_len)[None, :]
    toep = jnp.where((lag >= 0)[:, :, None, None, None], kern[jnp.clip(lag, 0)], 0.0)
    toep = toep.transpose(2, 1, 4, 0, 3).reshape(g, t_len * hch, t_len * hch)
    rp_re = pw_re[:t_len][::-1]
    rp_im = pw_im[:t_len][::-1]
    rr = rp_re[..., None] * bb_re[None] - rp_im[..., None] * bb_im[None]
    ri = rp_re[..., None] * bb_im[None] + rp_im[..., None] * bb_re[None]
    rr = rr.transpose(1, 0, 3, 2).reshape(g, t_len * hch, p)
    ri = ri.transpose(1, 0, 3, 2).reshape(g, t_len * hch, p)
    orr = ca_re[1:].transpose(1, 3, 0, 2).reshape(g, p, t_len * hch)
    oii = (-ca_im[1:]).transpose(1, 3, 0, 2).reshape(g, p, t_len * hch)

    def pair_diag(m):
        m = m.reshape(g // 2, 2, m.shape[1], m.shape[2])
        z = jnp.zeros_like(m[:, 0])
        top = jnp.concatenate([m[:, 0], z], axis=2)
        bot = jnp.concatenate([z, m[:, 1]], axis=2)
        return jnp.concatenate([top, bot], axis=1).astype(BF16)

    a_t_re = pw_re[t_len].reshape(g // 2, 1, 2 * p)
    a_t_im = pw_im[t_len].reshape(g // 2, 1, 2 * p)
    return pair_diag(toep), pair_diag(rr), pair_diag(ri), pair_diag(orr), pair_diag(oii), a_t_re, a_t_im


def _mixout_kernel(y_ref, a_ref, x_ref, mod_ref, wglu_ref, bglu_ref, ag_ref, sg_ref, wout_ref, o_ref):
    d_ssm = y_ref.shape[-1]
    d_attn = a_ref.shape[-1]
    gl = _dot(_gelu(y_ref[0]).astype(BF16), wglu_ref[...]) + bglu_ref[...]
    s = gl[:, :d_ssm] * _sigmoid(gl[:, d_ssm:])
    sn = _rms(s, sg_ref[...]).astype(BF16)
    an = _rms(a_ref[0].astype(F32), ag_ref[...]).astype(BF16)
    m = _dot(an, wout_ref[:d_attn, :]) + _dot(sn, wout_ref[d_attn:, :])
    o_ref[0] = x_ref[0] + mod_ref[0, 2:3, :] * m


def _mixout(y, a, x, mod, w_glu, b_glu, attn_g, ssm_g, w_out, tm):
    bsz, seq, d = x.shape
    d_ssm, d_attn = y.shape[-1], a.shape[-1]
    const = lambda shape: pl.BlockSpec(shape, lambda b, i: (0,) * len(shape))
    tile = lambda w: pl.BlockSpec((1, tm, w), lambda b, i: (b, i, 0))
    return pl.pallas_call(
        _mixout_kernel,
        grid=(bsz, seq // tm),
        in_specs=[tile(d_ssm), tile(d_attn), tile(d), pl.BlockSpec((1, 6, d), lambda b, i: (b, 0, 0)),
                  const(w_glu.shape), const((1, 2 * d_ssm)), const((1, d_attn)), const((1, d_ssm)),
                  const(w_out.shape)],
        out_specs=tile(d),
        out_shape=jax.ShapeDtypeStruct((bsz, seq, d), F32),
        compiler_params=_params("arbitrary", "arbitrary"),
        name="mixout",
    )(y, a, x, mod, w_glu, b_glu.reshape(1, -1), attn_g.reshape(1, -1), ssm_g.reshape(1, -1), w_out)


def _ffn_kernel(x_ref, halo_ref, mod_ref, ln_ref, wup_ref, cw_ref, cb_ref, wdn_ref, fg_ref, o_ref,
                gate_sc, acc_sc, *, tm, n_chunks, last):
    x = x_ref[0]
    xe = jnp.concatenate([halo_ref[0], x], axis=0)
    he = (_rms(xe, ln_ref[...]) * (1.0 + mod_ref[0, 4:5, :]) + mod_ref[0, 3:4, :]).astype(BF16)
    keep = (pl.program_id(1) > 0).astype(F32)
    acc_sc[...] = jnp.zeros(acc_sc.shape, F32)

    def chunk(j, carry):
        up = _dot(he, wup_ref[j])
        gate_sc[:HALO, :] = up[:HALO, :FF_CHUNK] * keep
        gate_sc[HALO:, :] = up[HALO:, :FF_CHUNK]
        cw = cw_ref[j]
        conv = cb_ref[j]
        for t in range(CONV_W):
            conv = conv + cw[t:t + 1, :] * gate_sc[pl.ds(HALO - (CONV_W - 1) + t, tm), :]
        act = (_gelu(conv) * up[HALO:, FF_CHUNK:]).astype(BF16)
        acc_sc[...] += _dot(act, wdn_ref[j])
        return carry

    lax.fori_loop(0, n_chunks, chunk, 0)
    x2 = x + mod_ref[0, 5:6, :] * acc_sc[...]
    o_ref[0] = _rms(x2, fg_ref[...]) if last else x2


def _ffn(x1, mod, ln2_g, w_up_r, conv_w_r, conv_b_r, w_down_r, final_g, tm, last):
    bsz, seq, d = x1.shape
    n_chunks = w_up_r.shape[0]
    const = lambda shape: pl.BlockSpec(shape, lambda b, i: (0,) * len(shape))
    hb = tm // HALO
    return pl.pallas_call(
        functools.partial(_ffn_kernel, tm=tm, n_chunks=n_chunks, last=last),
        grid=(bsz, seq // tm),
        in_specs=[pl.BlockSpec((1, tm, d), lambda b, i: (b, i, 0)),
                  pl.BlockSpec((1, HALO, d), lambda b, i: (b, jnp.maximum(i * hb - 1, 0), 0)),
                  pl.BlockSpec((1, 6, d), lambda b, i: (b, 0, 0)),
                  const((1, d)), const(w_up_r.shape), const(conv_w_r.shape), const(conv_b_r.shape),
                  const(w_down_r.shape), const((1, d))],
        out_specs=pl.BlockSpec((1, tm, d), lambda b, i: (b, i, 0)),
        out_shape=jax.ShapeDtypeStruct((bsz, seq, d), F32),
        scratch_shapes=[pltpu.VMEM((tm + HALO, FF_CHUNK), F32), pltpu.VMEM((tm, d), F32)],
        compiler_params=_params("arbitrary", "arbitrary"),
        name="ffn",
    )(x1, x1, mod, ln2_g.reshape(1, d), w_up_r, conv_w_r, conv_b_r, w_down_r, final_g.reshape(1, d))


def _layer(x, mod, cos, sin, w_in, ln1_g, q_norm_g, w_uq, kv_norm_g, w_ukv, ssm, w_glu, b_glu,
           attn_out_g, ssm_out_g, w_out, ln2_g, w_up, conv_w, conv_b, w_down, final_g, last):
    bsz, seq, d = x.shape
    tm = min(512, seq)
    half = QK_ROPE // 2
    o_kr = Q_LORA + KV_LORA
    o_u = o_kr + QK_ROPE
    d_ssm = w_in.shape[1] - o_u
    kr_w = w_in[:, o_kr:o_u]
    w_in_r = jnp.concatenate([w_in[:, :o_kr], w_in[:, o_u:], kr_w, kr_w], axis=1).astype(BF16)
    uq = w_uq.reshape(Q_LORA, N_HEADS, QK_NOPE + QK_ROPE)
    w_uq_r = jnp.concatenate([uq, uq[:, :, QK_NOPE:]], axis=2).reshape(Q_LORA, N_HEADS * HEAD_PAD).astype(BF16)
    ukv = w_ukv.reshape(KV_LORA, N_HEADS, QK_NOPE + V_HEAD)
    w_ukv_r = jnp.concatenate([ukv[:, :, :QK_NOPE].reshape(KV_LORA, -1),
                               ukv[:, :, QK_NOPE:].reshape(KV_LORA, -1)], axis=1).astype(BF16)
    zpad = jnp.zeros(cos.shape[:2] + (128 - 2 * half,), F32)
    cc = jnp.concatenate([cos, cos, zpad], axis=-1)
    ss = jnp.concatenate([-sin, sin, zpad], axis=-1)

    q, k, v, u = _inproj(x, mod, ln1_g, w_in_r, q_norm_g, w_uq_r, kv_norm_g, w_ukv_r, cc, ss, tm)
    a = _attention(q, k, v, tm)

    g = d_ssm // SSM_GROUP
    nck = seq // SSM_CHUNK
    ug = u.astype(BF16).reshape(bsz, nck, SSM_CHUNK, g // 2, 2, SSM_GROUP)
    ug = ug.transpose(3, 1, 0, 4, 2, 5).reshape(g // 2, nck * bsz, 2 * SSM_CHUNK * SSM_GROUP)
    yg = _ssm(ug, *ssm, bsz)
    y = yg.reshape(g // 2, nck, bsz, 2, SSM_CHUNK, SSM_GROUP).transpose(2, 1, 4, 0, 3, 5).reshape(bsz, seq, d_ssm)

    x1 = _mixout(y, a, x, mod, w_glu.astype(BF16), b_glu, attn_out_g, ssm_out_g, w_out.astype(BF16), tm)

    d_ff = w_down.shape[0]
    nch = d_ff // FF_CHUNK
    w_up_r = jnp.concatenate([w_up[:, :d_ff].reshape(d, nch, FF_CHUNK),
                              w_up[:, d_ff:].reshape(d, nch, FF_CHUNK)], axis=2).transpose(1, 0, 2).astype(BF16)
    conv_w_r = conv_w.reshape(CONV_W, nch, FF_CHUNK).transpose(1, 0, 2)
    conv_b_r = conv_b.reshape(nch, 1, FF_CHUNK)
    w_down_r = w_down.reshape(nch, FF_CHUNK, d).astype(BF16)
    return _ffn(x1, mod, ln2_g, w_up_r, conv_w_r, conv_b_r, w_down_r, final_g, tm, last)


def kernel(x, c, positions, w_mod, b_mod, ln1_g, w_in, q_norm_g, w_uq, kv_norm_g, w_ukv, ssm_lam_re, ssm_lam_im, ssm_log_dt, ssm_b_re, ssm_b_im, ssm_c_re, ssm_c_im, ssm_d, w_glu, b_glu, attn_out_g, ssm_out_g, w_out, ln2_g, w_up, conv_w, conv_b, w_down, final_g):
    bsz, seq, d = x.shape
    cos, sin = _rope_tables(positions)
    depth = w_in.shape[0]
    for l in range(depth):
        mod = _mod(c, w_mod[l], b_mod[l]).reshape(bsz, 6, d)
        ssm = _ssm_operators(ssm_lam_re[l], ssm_lam_im[l], ssm_log_dt[l], ssm_b_re[l], ssm_b_im[l],
                             ssm_c_re[l], ssm_c_im[l], ssm_d[l])
        x = _layer(x, mod, cos, sin, w_in[l], ln1_g[l], q_norm_g[l], w_uq[l], kv_norm_g[l], w_ukv[l], ssm,
                   w_glu[l], b_glu[l], attn_out_g[l], ssm_out_g[l], w_out[l], ln2_g[l], w_up[l], conv_w[l],
                   conv_b[l], w_down[l], final_g, l == depth - 1)
    return x
```

```python
import functools
import math

import jax
import jax.numpy as jnp
from jax import lax
from jax.experimental import pallas as pl
from jax.experimental.pallas import tpu as pltpu

N_HEADS = 4
QK_NOPE = 128
QK_ROPE = 64
V_HEAD = 128
Q_LORA = 384
KV_LORA = 256
ROPE_THETA = 10000.0
SSM_GROUP = 16
SSM_STATE = 64
CONV_W = 3
EPS = 1e-6

HEAD_PAD = 256
V_PAD = V_HEAD + 16
ATT_BLK = 256
ATT_CHAINS = 8
SSM_CHUNK = 16
SSM_TOK = 2048
FF_CHUNK = 256
HALO = 8
VMEM_LIMIT = 56 * 1024 * 1024

F32 = jnp.float32
BF16 = jnp.bfloat16


def _rms(x, g):
    return x * lax.rsqrt(jnp.mean(x * x, axis=-1, keepdims=True) + EPS) * g


def _gelu(x):
    return 0.5 * x * (1.0 + jnp.tanh(math.sqrt(2.0 / math.pi) * (x + 0.044715 * (x * x * x))))


def _sigmoid(x):
    return 1.0 / (1.0 + jnp.exp(-x))


def _dot(a, b):
    return jnp.dot(a, b, preferred_element_type=F32)


def _params(*sem, flags=None):
    return pltpu.CompilerParams(dimension_semantics=sem, vmem_limit_bytes=VMEM_LIMIT, flags=flags)


def _mod_kernel(c_ref, w_ref, b_ref, o_ref):
    c = c_ref[...]
    cond = c * _sigmoid(c)
    o_ref[...] = jnp.dot(cond, w_ref[...], preferred_element_type=F32,
                         precision=lax.Precision.HIGHEST) + b_ref[...]


def _mod(c, w_mod, b_mod):
    bsz, d = c.shape
    n = w_mod.shape[1]
    tn = 1024
    return pl.pallas_call(
        _mod_kernel,
        grid=(n // tn,),
        in_specs=[pl.BlockSpec((bsz, d), lambda j: (0, 0)),
                  pl.BlockSpec((d, tn), lambda j: (0, j)),
                  pl.BlockSpec((1, tn), lambda j: (0, j))],
        out_specs=pl.BlockSpec((bsz, tn), lambda j: (0, j)),
        out_shape=jax.ShapeDtypeStruct((bsz, n), F32),
        compiler_params=_params("arbitrary"),
        name="mod",
    )(c, w_mod, b_mod.reshape(1, n))


def _rope_t_kernel(pos_ref, f_ref, cos_ref, sin_ref):
    ang = f_ref[...] * pos_ref[0]
    cos_ref[0] = jnp.cos(ang)
    sin_ref[0] = jnp.sin(ang)


def _rope_tables_t(positions):
    bsz, seq = positions.shape
    half = QK_ROPE // 2
    inv_freq = ROPE_THETA ** (-jnp.arange(0, QK_ROPE, 2, dtype=F32) / QK_ROPE)
    ts = min(seq, 2048)
    return pl.pallas_call(
        _rope_t_kernel,
        grid=(bsz, seq // ts),
        in_specs=[pl.BlockSpec((1, 1, ts), lambda b, i: (b, 0, i)),
                  pl.BlockSpec((half, 1), lambda b, i: (0, 0))],
        out_specs=[pl.BlockSpec((1, half, ts), lambda b, i: (b, 0, i))] * 2,
        out_shape=[jax.ShapeDtypeStruct((bsz, half, seq), F32)] * 2,
        compiler_params=_params("arbitrary", "arbitrary"),
        name="rope_t",
    )(positions.astype(F32).reshape(bsz, 1, seq), inv_freq.reshape(half, 1))


_NT = (((1,), (1,)), ((), ()))


def _inproj_kernel(x_ref, mod_ref, ln_ref, win_ref, qg_ref, wuqt_ref, kvg_ref, wuk_ref, wuvt_ref,
                   cost_ref, sint_ref, qt_ref, k_ref, vt_ref, u_ref, *, scale, tm):
    x = x_ref[0]
    h = _rms(x, ln_ref[...]) * (1.0 + mod_ref[0, 1:2, :]) + mod_ref[0, 0:1, :]
    z = _dot(h.astype(BF16), win_ref[...])
    o_kv = Q_LORA
    o_u = o_kv + KV_LORA
    o_kr = z.shape[1] - 128
    for slab in range(u_ref.shape[1]):
        u_ref[0, slab] = z[:, o_u + 128 * slab:o_u + 128 * (slab + 1)]
    zqn = _rms(z[:, :o_kv], qg_ref[...]).astype(BF16)
    zkvn = _rms(z[:, o_kv:o_u], kvg_ref[...]).astype(BF16)
    qt = lax.dot_general(wuqt_ref[...], zqn, _NT, preferred_element_type=F32)
    vt = lax.dot_general(wuvt_ref[...], zkvn, _NT, preferred_element_type=F32)
    kn = _dot(zkvn, wuk_ref[...])
    cos_t = cost_ref[0]
    sin_t = sint_ref[0]
    half = QK_ROPE // 2
    zero_t = jnp.zeros((128 - QK_ROPE, tm), F32)
    cc = jnp.concatenate([cos_t, cos_t, zero_t], axis=0).T
    ss = jnp.concatenate([-sin_t, sin_t, zero_t], axis=0).T
    zkr = z[:, o_kr:]
    kr = (zkr * cc + pltpu.roll(zkr, half, 1) * ss).astype(BF16)
    ones_rows = (lax.broadcasted_iota(jnp.int32, (V_PAD - V_HEAD, tm), 0) == 0).astype(F32)
    for hd in range(N_HEADS):
        base = HEAD_PAD * hd
        r1 = qt[base + QK_NOPE:base + QK_NOPE + half]
        r2 = qt[base + QK_NOPE + half:base + QK_NOPE + QK_ROPE]
        head = jnp.concatenate([qt[base:base + QK_NOPE], r1 * cos_t - r2 * sin_t, r2 * cos_t + r1 * sin_t,
                                qt[base + QK_NOPE + QK_ROPE:base + HEAD_PAD]], axis=0)
        head = (head * scale).astype(BF16)
        vth = jnp.concatenate([vt[V_HEAD * hd:V_HEAD * (hd + 1)], ones_rows], axis=0).astype(BF16)
        for jj in range(tm // ATT_BLK):
            qt_ref[0, hd, jj] = head[:, ATT_BLK * jj:ATT_BLK * (jj + 1)]
            vt_ref[0, hd, jj] = vth[:, ATT_BLK * jj:ATT_BLK * (jj + 1)]
        k_ref[0, hd, :, :QK_NOPE] = kn[:, QK_NOPE * hd:QK_NOPE * (hd + 1)].astype(BF16)
        k_ref[0, hd, :, QK_NOPE:] = kr


def _inproj(x, mod, ln1_g, w_in_r, q_norm_g, w_uqt, kv_norm_g, w_uk, w_uvt, cos_t, sin_t, tm):
    bsz, seq, d = x.shape
    d_ssm = w_in_r.shape[1] - Q_LORA - KV_LORA - 128
    half = QK_ROPE // 2
    nb = tm // ATT_BLK
    const = lambda shape: pl.BlockSpec(shape, lambda b, i: (0,) * len(shape))
    scale = (QK_NOPE + QK_ROPE) ** -0.5 * math.log2(math.e)
    return pl.pallas_call(
        functools.partial(_inproj_kernel, scale=scale, tm=tm),
        grid=(bsz, seq // tm),
        in_specs=[pl.BlockSpec((1, tm, d), lambda b, i: (b, i, 0)),
                  pl.BlockSpec((1, 6, d), lambda b, i: (b, 0, 0)),
                  const((1, d)), const(w_in_r.shape), const((1, Q_LORA)), const(w_uqt.shape),
                  const((1, KV_LORA)), const(w_uk.shape), const(w_uvt.shape),
                  pl.BlockSpec((1, half, tm), lambda b, i: (b, 0, i)),
                  pl.BlockSpec((1, half, tm), lambda b, i: (b, 0, i))],
        out_specs=[pl.BlockSpec((1, N_HEADS, nb, HEAD_PAD, ATT_BLK), lambda b, i: (b, 0, i, 0, 0)),
                   pl.BlockSpec((1, N_HEADS, tm, HEAD_PAD), lambda b, i: (b, 0, i, 0)),
                   pl.BlockSpec((1, N_HEADS, nb, V_PAD, ATT_BLK), lambda b, i: (b, 0, i, 0, 0)),
                   pl.BlockSpec((1, d_ssm // 128, tm, 128), lambda b, i: (b, 0, i, 0))],
        out_shape=[jax.ShapeDtypeStruct((bsz, N_HEADS, seq // ATT_BLK, HEAD_PAD, ATT_BLK), BF16),
                   jax.ShapeDtypeStruct((bsz, N_HEADS, seq, HEAD_PAD), BF16),
                   jax.ShapeDtypeStruct((bsz, N_HEADS, seq // ATT_BLK, V_PAD, ATT_BLK), BF16),
                   jax.ShapeDtypeStruct((bsz, d_ssm // 128, seq, 128), F32)],
        compiler_params=_params("arbitrary", "arbitrary"),
        name="inproj",
    )(x, mod, ln1_g.reshape(1, d), w_in_r, q_norm_g.reshape(1, -1), w_uqt,
      kv_norm_g.reshape(1, -1), w_uk, w_uvt, cos_t, sin_t)


def _attn_kernel(qt_ref, k_ref, vt_ref, o_ref, m_sc, acc_sc, s_sc, *, blk, nq, dv):
    qi = pl.program_id(2)
    m_sc[...] = jnp.full(m_sc.shape, -jnp.inf, F32)
    acc_sc[...] = jnp.zeros(acc_sc.shape, F32)

    def scores(j, chains):
        k = k_ref[0, 0, pl.ds(pl.multiple_of(j * blk, blk), blk), :]
        return [_dot(k, qt_ref[0, 0, c]) for c in chains]

    def softmax_values(j, chains, s_list, diag):
        vt = vt_ref[0, 0, j]
        probs = []
        for c, s in zip(chains, s_list):
            if c == diag:
                key = lax.broadcasted_iota(jnp.int32, s.shape, 0)
                qry = lax.broadcasted_iota(jnp.int32, s.shape, 1)
                s = jnp.where(key <= qry, s, -1e30)
            m = m_sc[c]
            m_new = jnp.maximum(m, jnp.max(s, axis=0, keepdims=True))
            alpha = jnp.exp2(m - m_new)
            p = jnp.exp2(s - m_new)
            m_sc[c] = m_new
            probs.append((alpha, p.astype(BF16)))
        for c, (alpha, p) in zip(chains, probs):
            acc_sc[c] = alpha * acc_sc[c] + _dot(vt, p)

    every = range(nq)
    nfull = qi * nq

    def step(j, slot):
        for c, s in zip(every, scores(j + 1, every)):
            s_sc[1 - slot, c] = s
        softmax_values(j, every, [s_sc[slot, c] for c in every], None)

    def body(i, carry):
        step(2 * i, 0)
        step(2 * i + 1, 1)
        return carry

    for c, s in zip(every, scores(0, every)):
        s_sc[0, c] = s
    lax.fori_loop(0, nfull // 2, body, 0)
    s_cur = [s_sc[0, c] for c in every]
    for kc in range(nq):
        s_next = scores(nfull + kc + 1, range(kc + 1, nq)) if kc + 1 < nq else None
        softmax_values(nfull + kc, range(kc, nq), s_cur, kc)
        s_cur = s_next
    for c in range(nq):
        acc = acc_sc[c]
        o_ref[0, blk * c:blk * (c + 1), :] = (acc[:dv] / acc[dv:dv + 1]).T.astype(o_ref.dtype)


def _attention(qt, k, vt):
    bsz, nh, nblk, dh, blk = qt.shape
    dvp = vt.shape[3]
    dv = V_HEAD
    seq = nblk * blk
    nq = min(ATT_CHAINS, nblk)
    assert nq % 2 == 0 and nblk % nq == 0
    return pl.pallas_call(
        functools.partial(_attn_kernel, blk=blk, nq=nq, dv=dv),
        grid=(bsz, nh, nblk // nq),
        in_specs=[pl.BlockSpec((1, 1, nq, dh, blk), lambda b, h, i: (b, h, i, 0, 0)),
                  pl.BlockSpec((1, 1, seq, dh), lambda b, h, i: (b, h, 0, 0)),
                  pl.BlockSpec((1, 1, nblk, dvp, blk), lambda b, h, i: (b, h, 0, 0, 0))],
        out_specs=pl.BlockSpec((1, nq * blk, dv), lambda b, h, i: (b, i, h)),
        out_shape=jax.ShapeDtypeStruct((bsz, seq, nh * dv), BF16),
        scratch_shapes=[pltpu.VMEM((nq, 1, blk), F32), pltpu.VMEM((nq, dvp, blk), F32),
                        pltpu.VMEM((2, nq, blk, blk), F32)],
        compiler_params=_params("arbitrary", "arbitrary", "arbitrary"),
        name="attention",
    )(qt, k, vt)


def _slab_rows(slab, g8, s):
    g = 8 * slab + g8
    return g // 2, (g % 2) * SSM_CHUNK * SSM_GROUP + SSM_GROUP * s


def _regroup_kernel(u_ref, z_ref):
    nslab = u_ref.shape[1]
    nc = u_ref.shape[2] // SSM_CHUNK
    for slab in range(nslab):
        for s in range(SSM_CHUNK):
            t = u_ref[0, slab, pl.ds(s, nc, stride=SSM_CHUNK), :].T.astype(BF16)
            for g8 in range(128 // SSM_GROUP):
                pair, row = _slab_rows(slab, g8, s)
                z_ref[pair, row:row + SSM_GROUP, :] = t[SSM_GROUP * g8:SSM_GROUP * (g8 + 1), :]


def _regroup(u):
    bsz, nslab, seq, _ = u.shape
    nck = seq // SSM_CHUNK
    npair = nslab * 128 // SSM_GROUP // 2
    width = 2 * SSM_CHUNK * SSM_GROUP
    nt = seq // SSM_TOK
    return pl.pallas_call(
        _regroup_kernel,
        grid=(bsz, nt),
        in_specs=[pl.BlockSpec((1, nslab, SSM_TOK, 128), lambda b, i: (b, 0, i, 0))],
        out_specs=pl.BlockSpec((npair, width, SSM_TOK // SSM_CHUNK), lambda b, i: (0, 0, b * nt + i)),
        out_shape=jax.ShapeDtypeStruct((npair, width, bsz * nck), BF16),
        compiler_params=_params("arbitrary", "arbitrary"),
        name="regroup",
    )(u)


def _ungroup_kernel(yt_ref, y_ref):
    nslab = y_ref.shape[1]
    nc = y_ref.shape[2] // SSM_CHUNK
    for slab in range(nslab):
        for s in range(SSM_CHUNK):
            pieces = []
            for g8 in range(128 // SSM_GROUP):
                pair, row = _slab_rows(slab, g8, s)
                pieces.append(yt_ref[pair, row:row + SSM_GROUP, :])
            y_ref[0, slab, pl.ds(s, nc, stride=SSM_CHUNK), :] = jnp.concatenate(pieces, axis=0).T


def _ungroup(yt, bsz):
    npair, width, ncol = yt.shape
    seq = ncol // bsz * SSM_CHUNK
    nslab = npair * 2 * SSM_GROUP // 128
    nt = seq // SSM_TOK
    return pl.pallas_call(
        _ungroup_kernel,
        grid=(bsz, nt),
        in_specs=[pl.BlockSpec((npair, width, SSM_TOK // SSM_CHUNK), lambda b, i: (0, 0, b * nt + i))],
        out_specs=pl.BlockSpec((1, nslab, SSM_TOK, 128), lambda b, i: (b, 0, i, 0)),
        out_shape=jax.ShapeDtypeStruct((bsz, nslab, seq, 128), F32),
        compiler_params=_params("arbitrary", "arbitrary"),
        name="ungroup",
    )(yt)


def _ssm_kernel(z_ref, toep_ref, r_ref, ore_ref, oim_ref, are_ref, aim_ref, y_ref, rt_sc, xp_sc,
                *, bsz, nck):
    ncol = bsz * nck
    cb = min(512, ncol)
    ns = are_ref.shape[-1]
    for c0 in range(0, ncol, cb):
        r = _dot(r_ref[0], z_ref[0, :, c0:c0 + cb])
        for q in range(0, cb, 128):
            rt_sc[0, c0 + q:c0 + q + 128, :] = r[:ns, q:q + 128].T
            rt_sc[1, c0 + q:c0 + q + 128, :] = r[ns:, q:q + 128].T
    a_re = jnp.broadcast_to(are_ref[0], (bsz, ns))
    a_im = jnp.broadcast_to(aim_ref[0], (bsz, ns))

    def step(c, carry):
        s_re, s_im = carry
        rows = pl.ds(c, bsz, stride=nck)
        xp_sc[0, rows, :] = s_re
        xp_sc[1, rows, :] = s_im
        n_re = a_re * s_re - a_im * s_im + rt_sc[0, rows, :]
        n_im = a_re * s_im + a_im * s_re + rt_sc[1, rows, :]
        return n_re, n_im

    zero = jnp.zeros((bsz, ns), F32)
    lax.fori_loop(0, nck, step, (zero, zero), unroll=4)
    for c0 in range(0, ncol, cb):
        y = _dot(toep_ref[0], z_ref[0, :, c0:c0 + cb])
        y += lax.dot_general(ore_ref[0], xp_sc[0, c0:c0 + cb, :].astype(BF16), _NT, preferred_element_type=F32)
        y += lax.dot_general(oim_ref[0], xp_sc[1, c0:c0 + cb, :].astype(BF16), _NT, preferred_element_type=F32)
        y_ref[0, :, c0:c0 + cb] = y


def _ssm(z, toep, r, o_re, o_im, a_re, a_im, bsz):
    npair, width, ncol = z.shape
    ns = a_re.shape[-1]
    blk = lambda a: pl.BlockSpec((1,) + a.shape[1:], lambda g: (g,) + (0,) * (a.ndim - 1))
    return pl.pallas_call(
        functools.partial(_ssm_kernel, bsz=bsz, nck=ncol // bsz),
        grid=(npair,),
        in_specs=[blk(z), blk(toep), blk(r), blk(o_re), blk(o_im), blk(a_re), blk(a_im)],
        out_specs=pl.BlockSpec((1, width, ncol), lambda g: (g, 0, 0)),
        out_shape=jax.ShapeDtypeStruct((npair, width, ncol), F32),
        scratch_shapes=[pltpu.VMEM((2, ncol, ns), F32)] * 2,
        compiler_params=_params("arbitrary"),
        name="ssm",
    )(z, toep, r, o_re, o_im, a_re, a_im)


def _ssm_operators(lam_re, lam_im, log_dt, b_re, b_im, c_re, c_im, d_skip):
    t_len = SSM_CHUNK
    g, p = lam_re.shape
    hch = b_re.shape[-1]
    lr = jnp.minimum(lam_re.astype(F32), -1e-4)
    li = lam_im.astype(F32)
    dt = jnp.exp(log_dt.astype(F32))[:, None]
    mag = jnp.exp(lr * dt)
    ab_re = mag * jnp.cos(li * dt)
    ab_im = mag * jnp.sin(li * dt)
    den = lr * lr + li * li
    nr, ni = ab_re - 1.0, ab_im
    z_re = ((nr * lr + ni * li) / den)[..., None]
    z_im = ((ni * lr - nr * li) / den)[..., None]
    br, bi = b_re.astype(F32), b_im.astype(F32)
    bb_re = z_re * br - z_im * bi
    bb_im = z_re * bi + z_im * br
    tau = jnp.arange(t_len + 1, dtype=F32)[:, None, None]
    pm = jnp.exp(tau * (lr * dt)[None])
    pw_re = pm * jnp.cos(tau * (li * dt)[None])
    pw_im = pm * jnp.sin(tau * (li * dt)[None])
    cr, ci = c_re.astype(F32), c_im.astype(F32)
    hi = lax.Precision.HIGHEST
    ca_re = cr[None] * pw_re[:, :, None, :] - ci[None] * pw_im[:, :, None, :]
    ca_im = cr[None] * pw_im[:, :, None, :] + ci[None] * pw_re[:, :, None, :]
    kern = (jnp.einsum('tgop,gph->tgoh', ca_re[:t_len], bb_re, precision=hi)
            - jnp.einsum('tgop,gph->tgoh', ca_im[:t_len], bb_im, precision=hi))
    kern = kern.at[0].add(jnp.eye(hch, dtype=F32)[None] * d_skip.astype(F32)[:, :, None])
    lag = jnp.arange(t_len)[:, None] - jnp.arange(t_len)[None, :]
    toep = jnp.where((lag >= 0)[:, :, None, None, None], kern[jnp.clip(lag, 0)], 0.0)
    toep = toep.transpose(2, 1, 4, 0, 3).reshape(g, t_len * hch, t_len * hch)
    rp_re = pw_re[:t_len][::-1]
    rp_im = pw_im[:t_len][::-1]
    rr = rp_re[..., None] * bb_re[None] - rp_im[..., None] * bb_im[None]
    ri = rp_re[..., None] * bb_im[None] + rp_im[..., None] * bb_re[None]
    rr = rr.transpose(1, 0, 3, 2).reshape(g, t_len * hch, p)
    ri = ri.transpose(1, 0, 3, 2).reshape(g, t_len * hch, p)
    orr = ca_re[1:].transpose(1, 3, 0, 2).reshape(g, p, t_len * hch)
    oii = (-ca_im[1:]).transpose(1, 3, 0, 2).reshape(g, p, t_len * hch)

    def pair_diag(m):
        m = m.reshape(g // 2, 2, m.shape[1], m.shape[2])
        z = jnp.zeros_like(m[:, 0])
        top = jnp.concatenate([m[:, 0], z], axis=2)
        bot = jnp.concatenate([z, m[:, 1]], axis=2)
        return jnp.concatenate([top, bot], axis=1).astype(BF16)

    a_t_re = pw_re[t_len].reshape(g // 2, 1, 2 * p)
    a_t_im = pw_im[t_len].reshape(g // 2, 1, 2 * p)
    swap = lambda m: jnp.swapaxes(m, 1, 2)
    r_t = jnp.concatenate([pair_diag(swap(rr)), pair_diag(swap(ri))], axis=1)
    return pair_diag(swap(toep)), r_t, pair_diag(swap(orr)), pair_diag(swap(oii)), a_t_re, a_t_im


def _mixout_kernel(y_ref, a_ref, x_ref, mod_ref, wglu_ref, bglu_ref, ag_ref, sg_ref, wout_ref, o_ref):
    d_ssm = y_ref.shape[1] * y_ref.shape[3]
    d_attn = a_ref.shape[-1]
    y = jnp.concatenate([y_ref[0, slab] for slab in range(y_ref.shape[1])], axis=-1)
    gl = _dot(_gelu(y).astype(BF16), wglu_ref[...]) + bglu_ref[...]
    s = gl[:, :d_ssm] * _sigmoid(gl[:, d_ssm:])
    sn = _rms(s, sg_ref[...]).astype(BF16)
    an = _rms(a_ref[0].astype(F32), ag_ref[...]).astype(BF16)
    m = _dot(an, wout_ref[:d_attn, :]) + _dot(sn, wout_ref[d_attn:, :])
    o_ref[0] = x_ref[0] + mod_ref[0, 2:3, :] * m


def _mixout(y, a, x, mod, w_glu, b_glu, attn_g, ssm_g, w_out, tm):
    bsz, seq, d = x.shape
    nslab = y.shape[1]
    d_ssm, d_attn = nslab * y.shape[3], a.shape[-1]
    const = lambda shape: pl.BlockSpec(shape, lambda b, i: (0,) * len(shape))
    tile = lambda w: pl.BlockSpec((1, tm, w), lambda b, i: (b, i, 0))
    return pl.pallas_call(
        _mixout_kernel,
        grid=(bsz, seq // tm),
        in_specs=[pl.BlockSpec((1, nslab, tm, 128), lambda b, i: (b, 0, i, 0)), tile(d_attn), tile(d), pl.BlockSpec((1, 6, d), lambda b, i: (b, 0, 0)),
                  const(w_glu.shape), const((1, 2 * d_ssm)), const((1, d_attn)), const((1, d_ssm)),
                  const(w_out.shape)],
        out_specs=tile(d),
        out_shape=jax.ShapeDtypeStruct((bsz, seq, d), F32),
        compiler_params=_params("arbitrary", "arbitrary"),
        name="mixout",
    )(y, a, x, mod, w_glu, b_glu.reshape(1, -1), attn_g.reshape(1, -1), ssm_g.reshape(1, -1), w_out)


def _ffn_kernel(x_ref, halo_ref, mod_ref, ln_ref, wup_ref, cw_ref, cb_ref, wdn_ref, fg_ref, o_ref,
                gate_sc, *, tm, n_chunks, last):
    x = x_ref[0]
    xe = jnp.concatenate([halo_ref[0], x], axis=0)
    he = (_rms(xe, ln_ref[...]) * (1.0 + mod_ref[0, 4:5, :]) + mod_ref[0, 3:4, :]).astype(BF16)
    keep = (pl.program_id(1) > 0).astype(F32)

    up = _dot(he, wup_ref[0])
    acc = None
    for j in range(n_chunks):
        up_next = _dot(he, wup_ref[j + 1]) if j + 1 < n_chunks else None
        gate = gate_sc.at[j % 2]
        gate[:HALO, :] = up[:HALO, :FF_CHUNK] * keep
        gate[HALO:, :] = up[HALO:, :FF_CHUNK]
        cw = cw_ref[j]
        conv = cb_ref[j]
        for t in range(CONV_W):
            conv = conv + cw[t:t + 1, :] * gate[pl.ds(HALO - (CONV_W - 1) + t, tm), :]
        act = (_gelu(conv) * up[HALO:, FF_CHUNK:]).astype(BF16)
        down = _dot(act, wdn_ref[j])
        acc = down if acc is None else acc + down
        up = up_next
    x2 = x + mod_ref[0, 5:6, :] * acc
    o_ref[0] = _rms(x2, fg_ref[...]) if last else x2


def _ffn(x1, mod, ln2_g, w_up_r, conv_w_r, conv_b_r, w_down_r, final_g, tm, last):
    bsz, seq, d = x1.shape
    n_chunks = w_up_r.shape[0]
    const = lambda shape: pl.BlockSpec(shape, lambda b, i: (0,) * len(shape))
    hb = tm // HALO
    return pl.pallas_call(
        functools.partial(_ffn_kernel, tm=tm, n_chunks=n_chunks, last=last),
        grid=(bsz, seq // tm),
        in_specs=[pl.BlockSpec((1, tm, d), lambda b, i: (b, i, 0)),
                  pl.BlockSpec((1, HALO, d), lambda b, i: (b, jnp.maximum(i * hb - 1, 0), 0)),
                  pl.BlockSpec((1, 6, d), lambda b, i: (b, 0, 0)),
                  const((1, d)), const(w_up_r.shape), const(conv_w_r.shape), const(conv_b_r.shape),
                  const(w_down_r.shape), const((1, d))],
        out_specs=pl.BlockSpec((1, tm, d), lambda b, i: (b, i, 0)),
        out_shape=jax.ShapeDtypeStruct((bsz, seq, d), F32),
        scratch_shapes=[pltpu.VMEM((2, tm + HALO, FF_CHUNK), F32)],
        compiler_params=_params("arbitrary", "arbitrary"),
        name="ffn",
    )(x1, x1, mod, ln2_g.reshape(1, d), w_up_r, conv_w_r, conv_b_r, w_down_r, final_g.reshape(1, d))


def _layer(x, mod, cos_t, sin_t, w_in, ln1_g, q_norm_g, w_uq, kv_norm_g, w_ukv, ssm, w_glu, b_glu,
           attn_out_g, ssm_out_g, w_out, ln2_g, w_up, conv_w, conv_b, w_down, final_g, last):
    bsz, seq, d = x.shape
    tm = min(512, seq)
    half = QK_ROPE // 2
    o_kr = Q_LORA + KV_LORA
    o_u = o_kr + QK_ROPE
    d_ssm = w_in.shape[1] - o_u
    kr_w = w_in[:, o_kr:o_u]
    w_in_r = jnp.concatenate([w_in[:, :o_kr], w_in[:, o_u:], kr_w, kr_w], axis=1).astype(BF16)
    uq = w_uq.reshape(Q_LORA, N_HEADS, QK_NOPE + QK_ROPE)
    uq = jnp.pad(uq, ((0, 0), (0, 0), (0, HEAD_PAD - QK_NOPE - QK_ROPE)))
    w_uqt = uq.reshape(Q_LORA, N_HEADS * HEAD_PAD).T.astype(BF16)
    ukv = w_ukv.reshape(KV_LORA, N_HEADS, QK_NOPE + V_HEAD)
    w_uk = ukv[:, :, :QK_NOPE].reshape(KV_LORA, -1).astype(BF16)
    w_uvt = ukv[:, :, QK_NOPE:].reshape(KV_LORA, -1).T.astype(BF16)

    qt, k, vt, u = _inproj(x, mod, ln1_g, w_in_r, q_norm_g, w_uqt, kv_norm_g, w_uk, w_uvt, cos_t, sin_t, tm)
    a = _attention(qt, k, vt)

    y = _ungroup(_ssm(_regroup(u), *ssm, bsz), bsz)

    x1 = _mixout(y, a, x, mod, w_glu.astype(BF16), b_glu, attn_out_g, ssm_out_g, w_out.astype(BF16), tm)

    d_ff = w_down.shape[0]
    nch = d_ff // FF_CHUNK
    w_up_r = jnp.concatenate([w_up[:, :d_ff].reshape(d, nch, FF_CHUNK),
                              w_up[:, d_ff:].reshape(d, nch, FF_CHUNK)], axis=2).transpose(1, 0, 2).astype(BF16)
    conv_w_r = conv_w.reshape(CONV_W, nch, FF_CHUNK).transpose(1, 0, 2)
    conv_b_r = conv_b.reshape(nch, 1, FF_CHUNK)
    w_down_r = w_down.reshape(nch, FF_CHUNK, d).astype(BF16)
    return _ffn(x1, mod, ln2_g, w_up_r, conv_w_r, conv_b_r, w_down_r, final_g, tm, last)


def kernel(x, c, positions, w_mod, b_mod, ln1_g, w_in, q_norm_g, w_uq, kv_norm_g, w_ukv, ssm_lam_re, ssm_lam_im, ssm_log_dt, ssm_b_re, ssm_b_im, ssm_c_re, ssm_c_im, ssm_d, w_glu, b_glu, attn_out_g, ssm_out_g, w_out, ln2_g, w_up, conv_w, conv_b, w_down, final_g):
    bsz, seq, d = x.shape
    cos_t, sin_t = _rope_tables_t(positions)
    depth = w_in.shape[0]
    for l in range(depth):
        mod = _mod(c, w_mod[l], b_mod[l]).reshape(bsz, 6, d)
        ssm = _ssm_operators(ssm_lam_re[l], ssm_lam_im[l], ssm_log_dt[l], ssm_b_re[l], ssm_b_im[l],
                             ssm_c_re[l], ssm_c_im[l], ssm_d[l])
        x = _layer(x, mod, cos_t, sin_t, w_in[l], ln1_g[l], q_norm_g[l], w_uq[l], kv_norm_g[l], w_ukv[l], ssm,
                   w_glu[l], b_glu[l], attn_out_g[l], ssm_out_g[l], w_out[l], ln2_g[l], w_up[l], conv_w[l],
                   conv_b[l], w_down[l], final_g, l == depth - 1)
    return x
```

```python
import functools
import math

import jax
import jax.numpy as jnp
from jax import lax
from jax.experimental import pallas as pl
from jax.experimental.pallas import tpu as pltpu

N_HEADS = 4
QK_NOPE = 128
QK_ROPE = 64
V_HEAD = 128
Q_LORA = 384
KV_LORA = 256
ROPE_THETA = 10000.0
SSM_GROUP = 16
SSM_STATE = 64
CONV_W = 3
EPS = 1e-6

HEAD_PAD = 256
V_PAD = V_HEAD + 16
ATT_BLK = 256
ATT_CHAINS = 8
SSM_CHUNK = 16
SSM_TOK = 2048
FF_CHUNK = 256
FF_GROUP = 6
MIX_SPLIT = 2
HALO = 8
VMEM_LIMIT = 56 * 1024 * 1024

F32 = jnp.float32
BF16 = jnp.bfloat16


def _rms(x, g):
    return x * lax.rsqrt(jnp.mean(x * x, axis=-1, keepdims=True) + EPS) * g


def _gelu(x):
    return 0.5 * x * (1.0 + jnp.tanh(math.sqrt(2.0 / math.pi) * (x + 0.044715 * (x * x * x))))


def _gelu_sigmoid(x):
    a = -2.0 * math.sqrt(2.0 / math.pi) * math.log2(math.e)
    return x * (1.0 / (1.0 + jnp.exp2(x * (a + (a * 0.044715) * (x * x)))))


def _sigmoid(x):
    return 1.0 / (1.0 + jnp.exp(-x))


def _dot(a, b):
    return jnp.dot(a, b, preferred_element_type=F32)


def _params(*sem, flags=None):
    return pltpu.CompilerParams(dimension_semantics=sem, vmem_limit_bytes=VMEM_LIMIT, flags=flags)


def _mod_kernel(c_ref, w_ref, b_ref, o_ref):
    c = c_ref[...]
    cond = c * _sigmoid(c)
    o_ref[...] = jnp.dot(cond, w_ref[...], preferred_element_type=F32,
                         precision=lax.Precision.HIGHEST) + b_ref[...]


def _mod(c, w_mod, b_mod):
    bsz, d = c.shape
    n = w_mod.shape[1]
    tn = 1024
    return pl.pallas_call(
        _mod_kernel,
        grid=(n // tn,),
        in_specs=[pl.BlockSpec((bsz, d), lambda j: (0, 0)),
                  pl.BlockSpec((d, tn), lambda j: (0, j)),
                  pl.BlockSpec((1, tn), lambda j: (0, j))],
        out_specs=pl.BlockSpec((bsz, tn), lambda j: (0, j)),
        out_shape=jax.ShapeDtypeStruct((bsz, n), F32),
        compiler_params=_params("arbitrary"),
        name="mod",
    )(c, w_mod, b_mod.reshape(1, n))


def _rope_t_kernel(pos_ref, f_ref, cos_ref, sin_ref):
    ang = f_ref[...] * pos_ref[0]
    cos_ref[0] = jnp.cos(ang)
    sin_ref[0] = jnp.sin(ang)


def _rope_tables_t(positions):
    bsz, seq = positions.shape
    half = QK_ROPE // 2
    inv_freq = ROPE_THETA ** (-jnp.arange(0, QK_ROPE, 2, dtype=F32) / QK_ROPE)
    ts = min(seq, 2048)
    return pl.pallas_call(
        _rope_t_kernel,
        grid=(bsz, seq // ts),
        in_specs=[pl.BlockSpec((1, 1, ts), lambda b, i: (b, 0, i)),
                  pl.BlockSpec((half, 1), lambda b, i: (0, 0))],
        out_specs=[pl.BlockSpec((1, half, ts), lambda b, i: (b, 0, i))] * 2,
        out_shape=[jax.ShapeDtypeStruct((bsz, half, seq), F32)] * 2,
        compiler_params=_params("arbitrary", "arbitrary"),
        name="rope_t",
    )(positions.astype(F32).reshape(bsz, 1, seq), inv_freq.reshape(half, 1))


_NT = (((1,), (1,)), ((), ()))


def _inproj_kernel(x_ref, mod_ref, ln_ref, win_ref, qg_ref, wuqt_ref, kvg_ref, wuk_ref, wuvt_ref,
                   cost_ref, sint_ref, qt_ref, k_ref, vt_ref, u_ref, *, scale, tm):
    x = x_ref[0]
    h = _rms(x, ln_ref[...]) * (1.0 + mod_ref[0, 1:2, :]) + mod_ref[0, 0:1, :]
    z = _dot(h.astype(BF16), win_ref[...])
    o_kv = Q_LORA
    o_u = o_kv + KV_LORA
    o_kr = z.shape[1] - 128
    for slab in range(u_ref.shape[1]):
        u_ref[0, slab] = z[:, o_u + 128 * slab:o_u + 128 * (slab + 1)]
    zqn = _rms(z[:, :o_kv], qg_ref[...]).astype(BF16)
    zkvn = _rms(z[:, o_kv:o_u], kvg_ref[...]).astype(BF16)
    qt = lax.dot_general(wuqt_ref[...], zqn, _NT, preferred_element_type=F32)
    vt = lax.dot_general(wuvt_ref[...], zkvn, _NT, preferred_element_type=F32)
    kn = _dot(zkvn, wuk_ref[...])
    cos_t = cost_ref[0]
    sin_t = sint_ref[0]
    half = QK_ROPE // 2
    zero_t = jnp.zeros((128 - QK_ROPE, tm), F32)
    cc = jnp.concatenate([cos_t, cos_t, zero_t], axis=0).T
    ss = jnp.concatenate([-sin_t, sin_t, zero_t], axis=0).T
    zkr = z[:, o_kr:]
    kr = (zkr * cc + pltpu.roll(zkr, half, 1) * ss).astype(BF16)
    ones_rows = (lax.broadcasted_iota(jnp.int32, (V_PAD - V_HEAD, tm), 0) == 0).astype(F32)
    for hd in range(N_HEADS):
        base = HEAD_PAD * hd
        r1 = qt[base + QK_NOPE:base + QK_NOPE + half]
        r2 = qt[base + QK_NOPE + half:base + QK_NOPE + QK_ROPE]
        head = jnp.concatenate([qt[base:base + QK_NOPE], r1 * cos_t - r2 * sin_t, r2 * cos_t + r1 * sin_t,
                                qt[base + QK_NOPE + QK_ROPE:base + HEAD_PAD]], axis=0)
        head = (head * scale).astype(BF16)
        vth = jnp.concatenate([vt[V_HEAD * hd:V_HEAD * (hd + 1)], ones_rows], axis=0).astype(BF16)
        for jj in range(tm // ATT_BLK):
            qt_ref[0, hd, jj] = head[:, ATT_BLK * jj:ATT_BLK * (jj + 1)]
            vt_ref[0, hd, jj] = vth[:, ATT_BLK * jj:ATT_BLK * (jj + 1)]
        k_ref[0, hd, :, :QK_NOPE] = kn[:, QK_NOPE * hd:QK_NOPE * (hd + 1)].astype(BF16)
        k_ref[0, hd, :, QK_NOPE:] = kr


def _inproj(x, mod, ln1_g, w_in_r, q_norm_g, w_uqt, kv_norm_g, w_uk, w_uvt, cos_t, sin_t, tm):
    bsz, seq, d = x.shape
    d_ssm = w_in_r.shape[1] - Q_LORA - KV_LORA - 128
    half = QK_ROPE // 2
    nb = tm // ATT_BLK
    const = lambda shape: pl.BlockSpec(shape, lambda b, i: (0,) * len(shape))
    scale = (QK_NOPE + QK_ROPE) ** -0.5 * math.log2(math.e)
    return pl.pallas_call(
        functools.partial(_inproj_kernel, scale=scale, tm=tm),
        grid=(bsz, seq // tm),
        in_specs=[pl.BlockSpec((1, tm, d), lambda b, i: (b, i, 0)),
                  pl.BlockSpec((1, 6, d), lambda b, i: (b, 0, 0)),
                  const((1, d)), const(w_in_r.shape), const((1, Q_LORA)), const(w_uqt.shape),
                  const((1, KV_LORA)), const(w_uk.shape), const(w_uvt.shape),
                  pl.BlockSpec((1, half, tm), lambda b, i: (b, 0, i)),
                  pl.BlockSpec((1, half, tm), lambda b, i: (b, 0, i))],
        out_specs=[pl.BlockSpec((1, N_HEADS, nb, HEAD_PAD, ATT_BLK), lambda b, i: (b, 0, i, 0, 0)),
                   pl.BlockSpec((1, N_HEADS, tm, HEAD_PAD), lambda b, i: (b, 0, i, 0)),
                   pl.BlockSpec((1, N_HEADS, nb, V_PAD, ATT_BLK), lambda b, i: (b, 0, i, 0, 0)),
                   pl.BlockSpec((1, d_ssm // 128, tm, 128), lambda b, i: (b, 0, i, 0))],
        out_shape=[jax.ShapeDtypeStruct((bsz, N_HEADS, seq // ATT_BLK, HEAD_PAD, ATT_BLK), BF16),
                   jax.ShapeDtypeStruct((bsz, N_HEADS, seq, HEAD_PAD), BF16),
                   jax.ShapeDtypeStruct((bsz, N_HEADS, seq // ATT_BLK, V_PAD, ATT_BLK), BF16),
                   jax.ShapeDtypeStruct((bsz, d_ssm // 128, seq, 128), F32)],
        compiler_params=_params("arbitrary", "arbitrary"),
        name="inproj",
    )(x, mod, ln1_g.reshape(1, d), w_in_r, q_norm_g.reshape(1, -1), w_uqt,
      kv_norm_g.reshape(1, -1), w_uk, w_uvt, cos_t, sin_t)


def _attn_kernel(qt_ref, k_ref, vt_ref, o_ref, m_sc, acc_sc, s_sc, *, blk, nq, dv):
    qi = pl.program_id(2)
    m_sc[...] = jnp.full(m_sc.shape, -jnp.inf, F32)
    acc_sc[...] = jnp.zeros(acc_sc.shape, F32)

    def scores(j, chains):
        k = k_ref[0, 0, pl.ds(pl.multiple_of(j * blk, blk), blk), :]
        return [_dot(k, qt_ref[0, 0, c]) for c in chains]

    def softmax_values(j, chains, s_list, diag):
        vt = vt_ref[0, 0, j]
        probs = []
        for c, s in zip(chains, s_list):
            if c == diag:
                key = lax.broadcasted_iota(jnp.int32, s.shape, 0)
                qry = lax.broadcasted_iota(jnp.int32, s.shape, 1)
                s = jnp.where(key <= qry, s, -1e30)
            m = m_sc[c]
            m_new = jnp.maximum(m, jnp.max(s, axis=0, keepdims=True))
            alpha = jnp.exp2(m - m_new)
            p = jnp.exp2(s - m_new)
            m_sc[c] = m_new
            probs.append((alpha, p.astype(BF16)))
        for c, (alpha, p) in zip(chains, probs):
            acc_sc[c] = alpha * acc_sc[c] + _dot(vt, p)

    every = range(nq)
    nfull = qi * nq

    def step(j, slot):
        for c, s in zip(every, scores(j + 1, every)):
            s_sc[1 - slot, c] = s
        softmax_values(j, every, [s_sc[slot, c] for c in every], None)

    def body(i, carry):
        step(2 * i, 0)
        step(2 * i + 1, 1)
        return carry

    for c, s in zip(every, scores(0, every)):
        s_sc[0, c] = s
    lax.fori_loop(0, nfull // 2, body, 0)
    s_cur = [s_sc[0, c] for c in every]
    for kc in range(nq):
        s_next = scores(nfull + kc + 1, range(kc + 1, nq)) if kc + 1 < nq else None
        softmax_values(nfull + kc, range(kc, nq), s_cur, kc)
        s_cur = s_next
    for c in range(nq):
        acc = acc_sc[c]
        o_ref[0, blk * c:blk * (c + 1), :] = (acc[:dv] / acc[dv:dv + 1]).T.astype(o_ref.dtype)


def _attention(qt, k, vt):
    bsz, nh, nblk, dh, blk = qt.shape
    dvp = vt.shape[3]
    dv = V_HEAD
    seq = nblk * blk
    nq = min(ATT_CHAINS, nblk)
    assert nq % 2 == 0 and nblk % nq == 0
    return pl.pallas_call(
        functools.partial(_attn_kernel, blk=blk, nq=nq, dv=dv),
        grid=(bsz, nh, nblk // nq),
        in_specs=[pl.BlockSpec((1, 1, nq, dh, blk), lambda b, h, i: (b, h, i, 0, 0)),
                  pl.BlockSpec((1, 1, seq, dh), lambda b, h, i: (b, h, 0, 0)),
                  pl.BlockSpec((1, 1, nblk, dvp, blk), lambda b, h, i: (b, h, 0, 0, 0))],
        out_specs=pl.BlockSpec((1, nq * blk, dv), lambda b, h, i: (b, i, h)),
        out_shape=jax.ShapeDtypeStruct((bsz, seq, nh * dv), BF16),
        scratch_shapes=[pltpu.VMEM((nq, 1, blk), F32), pltpu.VMEM((nq, dvp, blk), F32),
                        pltpu.VMEM((2, nq, blk, blk), F32)],
        compiler_params=_params("arbitrary", "arbitrary", "arbitrary"),
        name="attention",
    )(qt, k, vt)


def _slab_rows(slab, g8, s):
    g = 8 * slab + g8
    return g // 2, (g % 2) * SSM_CHUNK * SSM_GROUP + SSM_GROUP * s


def _regroup_kernel(u_ref, z_ref):
    nslab = u_ref.shape[1]
    nc = u_ref.shape[2] // SSM_CHUNK
    for slab in range(nslab):
        for s in range(SSM_CHUNK):
            t = u_ref[0, slab, pl.ds(s, nc, stride=SSM_CHUNK), :].T.astype(BF16)
            for g8 in range(128 // SSM_GROUP):
                pair, row = _slab_rows(slab, g8, s)
                z_ref[pair, row:row + SSM_GROUP, :] = t[SSM_GROUP * g8:SSM_GROUP * (g8 + 1), :]


def _regroup(u):
    bsz, nslab, seq, _ = u.shape
    nck = seq // SSM_CHUNK
    npair = nslab * 128 // SSM_GROUP // 2
    width = 2 * SSM_CHUNK * SSM_GROUP
    nt = seq // SSM_TOK
    return pl.pallas_call(
        _regroup_kernel,
        grid=(bsz, nt),
        in_specs=[pl.BlockSpec((1, nslab, SSM_TOK, 128), lambda b, i: (b, 0, i, 0))],
        out_specs=pl.BlockSpec((npair, width, SSM_TOK // SSM_CHUNK), lambda b, i: (0, 0, b * nt + i)),
        out_shape=jax.ShapeDtypeStruct((npair, width, bsz * nck), BF16),
        compiler_params=_params("arbitrary", "arbitrary"),
        name="regroup",
    )(u)


def _ungroup_kernel(yt_ref, y_ref):
    nslab = y_ref.shape[1]
    nc = y_ref.shape[2] // SSM_CHUNK
    for slab in range(nslab):
        for s in range(SSM_CHUNK):
            pieces = []
            for g8 in range(128 // SSM_GROUP):
                pair, row = _slab_rows(slab, g8, s)
                pieces.append(yt_ref[pair, row:row + SSM_GROUP, :])
            y_ref[0, slab, pl.ds(s, nc, stride=SSM_CHUNK), :] = jnp.concatenate(pieces, axis=0).T


def _ungroup(yt, bsz):
    npair, width, ncol = yt.shape
    seq = ncol // bsz * SSM_CHUNK
    nslab = npair * 2 * SSM_GROUP // 128
    nt = seq // SSM_TOK
    return pl.pallas_call(
        _ungroup_kernel,
        grid=(bsz, nt),
        in_specs=[pl.BlockSpec((npair, width, SSM_TOK // SSM_CHUNK), lambda b, i: (0, 0, b * nt + i))],
        out_specs=pl.BlockSpec((1, nslab, SSM_TOK, 128), lambda b, i: (b, 0, i, 0)),
        out_shape=jax.ShapeDtypeStruct((bsz, nslab, seq, 128), F32),
        compiler_params=_params("arbitrary", "arbitrary"),
        name="ungroup",
    )(yt)


def _ssm_kernel(z_ref, toep_ref, r_ref, ore_ref, oim_ref, are_ref, aim_ref, y_ref, rt_sc, xp_sc,
                *, bsz, nck):
    ncol = bsz * nck
    cb = min(512, ncol)
    ns = are_ref.shape[-1]
    for c0 in range(0, ncol, cb):
        r = _dot(r_ref[0], z_ref[0, :, c0:c0 + cb])
        for q in range(0, cb, 128):
            rt_sc[0, c0 + q:c0 + q + 128, :] = r[:ns, q:q + 128].T
            rt_sc[1, c0 + q:c0 + q + 128, :] = r[ns:, q:q + 128].T
    a_re = jnp.broadcast_to(are_ref[0], (bsz, ns))
    a_im = jnp.broadcast_to(aim_ref[0], (bsz, ns))

    def step(c, carry):
        s_re, s_im = carry
        rows = pl.ds(c, bsz, stride=nck)
        xp_sc[0, rows, :] = s_re
        xp_sc[1, rows, :] = s_im
        n_re = a_re * s_re - a_im * s_im + rt_sc[0, rows, :]
        n_im = a_re * s_im + a_im * s_re + rt_sc[1, rows, :]
        return n_re, n_im

    zero = jnp.zeros((bsz, ns), F32)
    lax.fori_loop(0, nck, step, (zero, zero), unroll=4)
    for c0 in range(0, ncol, cb):
        y = _dot(toep_ref[0], z_ref[0, :, c0:c0 + cb])
        y += lax.dot_general(ore_ref[0], xp_sc[0, c0:c0 + cb, :].astype(BF16), _NT, preferred_element_type=F32)
        y += lax.dot_general(oim_ref[0], xp_sc[1, c0:c0 + cb, :].astype(BF16), _NT, preferred_element_type=F32)
        y_ref[0, :, c0:c0 + cb] = y


def _ssm(z, toep, r, o_re, o_im, a_re, a_im, bsz):
    npair, width, ncol = z.shape
    ns = a_re.shape[-1]
    blk = lambda a: pl.BlockSpec((1,) + a.shape[1:], lambda g: (g,) + (0,) * (a.ndim - 1))
    return pl.pallas_call(
        functools.partial(_ssm_kernel, bsz=bsz, nck=ncol // bsz),
        grid=(npair,),
        in_specs=[blk(z), blk(toep), blk(r), blk(o_re), blk(o_im), blk(a_re), blk(a_im)],
        out_specs=pl.BlockSpec((1, width, ncol), lambda g: (g, 0, 0)),
        out_shape=jax.ShapeDtypeStruct((npair, width, ncol), F32),
        scratch_shapes=[pltpu.VMEM((2, ncol, ns), F32)] * 2,
        compiler_params=_params("arbitrary"),
        name="ssm",
    )(z, toep, r, o_re, o_im, a_re, a_im)


def _ssm_operators(lam_re, lam_im, log_dt, b_re, b_im, c_re, c_im, d_skip):
    t_len = SSM_CHUNK
    g, p = lam_re.shape
    hch = b_re.shape[-1]
    lr = jnp.minimum(lam_re.astype(F32), -1e-4)
    li = lam_im.astype(F32)
    dt = jnp.exp(log_dt.astype(F32))[:, None]
    mag = jnp.exp(lr * dt)
    ab_re = mag * jnp.cos(li * dt)
    ab_im = mag * jnp.sin(li * dt)
    den = lr * lr + li * li
    nr, ni = ab_re - 1.0, ab_im
    z_re = ((nr * lr + ni * li) / den)[..., None]
    z_im = ((ni * lr - nr * li) / den)[..., None]
    br, bi = b_re.astype(F32), b_im.astype(F32)
    bb_re = z_re * br - z_im * bi
    bb_im = z_re * bi + z_im * br
    tau = jnp.arange(t_len + 1, dtype=F32)[None, :, None]
    pm = jnp.exp(tau * (lr * dt)[:, None, :])
    pw_re = pm * jnp.cos(tau * (li * dt)[:, None, :])
    pw_im = pm * jnp.sin(tau * (li * dt)[:, None, :])
    cr, ci = c_re.astype(F32)[:, None], c_im.astype(F32)[:, None]
    ca_re = cr * pw_re[:, :, None, :] - ci * pw_im[:, :, None, :]
    ca_im = cr * pw_im[:, :, None, :] + ci * pw_re[:, :, None, :]
    hi = lax.Precision.HIGHEST
    kern = (jnp.einsum('gtop,gph->goth', ca_re[:, :t_len], bb_re, precision=hi)
            - jnp.einsum('gtop,gph->goth', ca_im[:, :t_len], bb_im, precision=hi))
    kern = kern.at[:, :, 0, :].add(jnp.eye(hch, dtype=F32)[None] * d_skip.astype(F32)[:, :, None])
    krow = jnp.concatenate([kern[:, :, ::-1, :].reshape(g, hch, t_len * hch),
                            jnp.zeros((g, hch, (t_len - 1) * hch), F32)], axis=-1)
    toep = jnp.stack([krow[:, :, (t_len - 1 - t) * hch:(2 * t_len - 1 - t) * hch] for t in range(t_len)], axis=1)
    toep = toep.reshape(g, t_len * hch, t_len * hch)
    rp_re = pw_re[:, :t_len][:, ::-1].transpose(0, 2, 1)[..., None]
    rp_im = pw_im[:, :t_len][:, ::-1].transpose(0, 2, 1)[..., None]
    rr = (rp_re * bb_re[:, :, None, :] - rp_im * bb_im[:, :, None, :]).reshape(g, p, t_len * hch)
    ri = (rp_re * bb_im[:, :, None, :] + rp_im * bb_re[:, :, None, :]).reshape(g, p, t_len * hch)
    orr = ca_re[:, 1:].reshape(g, t_len * hch, p)
    oii = (-ca_im[:, 1:]).reshape(g, t_len * hch, p)

    def pair_diag(m):
        m = m.reshape(g // 2, 2, m.shape[1], m.shape[2])
        z = jnp.zeros_like(m[:, 0])
        top = jnp.concatenate([m[:, 0], z], axis=2)
        bot = jnp.concatenate([z, m[:, 1]], axis=2)
        return jnp.concatenate([top, bot], axis=1).astype(BF16)

    a_t_re = pw_re[:, t_len].reshape(g // 2, 1, 2 * p)
    a_t_im = pw_im[:, t_len].reshape(g // 2, 1, 2 * p)
    r_t = jnp.concatenate([pair_diag(rr), pair_diag(ri)], axis=1)
    return pair_diag(toep), r_t, pair_diag(orr), pair_diag(oii), a_t_re, a_t_im


def _mixout_kernel(y_ref, a_ref, x_ref, mod_ref, wglu_ref, bglu_ref, ag_ref, sg_ref, wout_ref, o_ref):
    d_ssm = y_ref.shape[1] * y_ref.shape[3]
    d_attn = a_ref.shape[-1]
    tm = x_ref.shape[1]
    parts = [slice(r, r + tm // MIX_SPLIT) for r in range(0, tm, tm // MIX_SPLIT)]
    gls = []
    for rows in parts:
        y = jnp.concatenate([y_ref[0, slab, rows, :] for slab in range(y_ref.shape[1])], axis=-1)
        gls.append(_dot(_gelu_sigmoid(y).astype(BF16), wglu_ref[...]) + bglu_ref[...])
    for rows, gl in zip(parts, gls):
        s = gl[:, :d_ssm] * _sigmoid(gl[:, d_ssm:])
        sn = _rms(s, sg_ref[...]).astype(BF16)
        an = _rms(a_ref[0, rows, :].astype(F32), ag_ref[...]).astype(BF16)
        m = _dot(an, wout_ref[:d_attn, :]) + _dot(sn, wout_ref[d_attn:, :])
        o_ref[0, rows, :] = x_ref[0, rows, :] + mod_ref[0, 2:3, :] * m


def _mixout(y, a, x, mod, w_glu, b_glu, attn_g, ssm_g, w_out, tm):
    bsz, seq, d = x.shape
    nslab = y.shape[1]
    d_ssm, d_attn = nslab * y.shape[3], a.shape[-1]
    const = lambda shape: pl.BlockSpec(shape, lambda b, i: (0,) * len(shape))
    tile = lambda w: pl.BlockSpec((1, tm, w), lambda b, i: (b, i, 0))
    return pl.pallas_call(
        _mixout_kernel,
        grid=(bsz, seq // tm),
        in_specs=[pl.BlockSpec((1, nslab, tm, 128), lambda b, i: (b, 0, i, 0)), tile(d_attn), tile(d), pl.BlockSpec((1, 6, d), lambda b, i: (b, 0, 0)),
                  const(w_glu.shape), const((1, 2 * d_ssm)), const((1, d_attn)), const((1, d_ssm)),
                  const(w_out.shape)],
        out_specs=tile(d),
        out_shape=jax.ShapeDtypeStruct((bsz, seq, d), F32),
        compiler_params=_params("arbitrary", "arbitrary"),
        name="mixout",
    )(y, a, x, mod, w_glu, b_glu.reshape(1, -1), attn_g.reshape(1, -1), ssm_g.reshape(1, -1), w_out)


def _ffn_kernel(x_ref, halo_ref, mod_ref, ln_ref, wup_ref, cw_ref, cb_ref, wdn_ref, fg_ref, o_ref,
                *, tm, n_chunks, last):
    x = x_ref[0]
    xe = jnp.concatenate([halo_ref[0], x], axis=0)
    he = (_rms(xe, ln_ref[...]) * (1.0 + mod_ref[0, 4:5, :]) + mod_ref[0, 3:4, :]).astype(BF16)
    keep = (pl.program_id(1) > 0).astype(F32)

    wup = lambda j: wup_ref[:, 2 * FF_CHUNK * j:2 * FF_CHUNK * (j + 1)]
    up = _dot(he, wup(0))
    acc = None
    acts = []
    for j in range(n_chunks):
        up_next = _dot(he, wup(j + 1)) if j + 1 < n_chunks else None
        gate = jnp.concatenate([up[:HALO, :FF_CHUNK] * keep, up[HALO:, :FF_CHUNK]], axis=0)
        cols = slice(FF_CHUNK * j, FF_CHUNK * (j + 1))
        conv = cw_ref[0:1, cols] * gate
        for t in range(1, CONV_W):
            conv = cw_ref[t:t + 1, cols] * gate + pltpu.roll(conv, 1, 0)
        conv = conv[HALO:] + cb_ref[:, cols]
        acts.append((_gelu_sigmoid(conv) * up[HALO:, FF_CHUNK:]).astype(BF16))
        if len(acts) == FF_GROUP or j + 1 == n_chunks:
            rows = slice(FF_CHUNK * (j + 1 - len(acts)), FF_CHUNK * (j + 1))
            down = _dot(jnp.concatenate(acts, axis=-1), wdn_ref[rows, :])
            acc = down if acc is None else acc + down
            acts = []
        up = up_next
    x2 = x + mod_ref[0, 5:6, :] * acc
    o_ref[0] = _rms(x2, fg_ref[...]) if last else x2


def _ffn(x1, mod, ln2_g, w_up_r, conv_w_r, conv_b_r, w_down_r, final_g, tm, last):
    bsz, seq, d = x1.shape
    n_chunks = w_up_r.shape[1] // (2 * FF_CHUNK)
    const = lambda shape: pl.BlockSpec(shape, lambda b, i: (0,) * len(shape))
    hb = tm // HALO
    return pl.pallas_call(
        functools.partial(_ffn_kernel, tm=tm, n_chunks=n_chunks, last=last),
        grid=(bsz, seq // tm),
        in_specs=[pl.BlockSpec((1, tm, d), lambda b, i: (b, i, 0)),
                  pl.BlockSpec((1, HALO, d), lambda b, i: (b, jnp.maximum(i * hb - 1, 0), 0)),
                  pl.BlockSpec((1, 6, d), lambda b, i: (b, 0, 0)),
                  const((1, d)), const(w_up_r.shape), const(conv_w_r.shape), const(conv_b_r.shape),
                  const(w_down_r.shape), const((1, d))],
        out_specs=pl.BlockSpec((1, tm, d), lambda b, i: (b, i, 0)),
        out_shape=jax.ShapeDtypeStruct((bsz, seq, d), F32),
        compiler_params=_params("arbitrary", "arbitrary"),
        name="ffn",
    )(x1, x1, mod, ln2_g.reshape(1, d), w_up_r, conv_w_r, conv_b_r, w_down_r, final_g.reshape(1, d))


def _layer(x, mod, cos_t, sin_t, w_in, ln1_g, q_norm_g, w_uq, kv_norm_g, w_ukv, ssm, w_glu, b_glu,
           attn_out_g, ssm_out_g, w_out, ln2_g, w_up, conv_w, conv_b, w_down, final_g, last):
    bsz, seq, d = x.shape
    tm = min(512, seq)
    half = QK_ROPE // 2
    o_kr = Q_LORA + KV_LORA
    o_u = o_kr + QK_ROPE
    d_ssm = w_in.shape[1] - o_u
    kr_w = w_in[:, o_kr:o_u]
    w_in_r = jnp.concatenate([w_in[:, :o_kr], w_in[:, o_u:], kr_w, kr_w], axis=1).astype(BF16)
    uq = w_uq.reshape(Q_LORA, N_HEADS, QK_NOPE + QK_ROPE)
    uq = jnp.pad(uq, ((0, 0), (0, 0), (0, HEAD_PAD - QK_NOPE - QK_ROPE)))
    w_uqt = uq.reshape(Q_LORA, N_HEADS * HEAD_PAD).T.astype(BF16)
    ukv = w_ukv.reshape(KV_LORA, N_HEADS, QK_NOPE + V_HEAD)
    w_uk = ukv[:, :, :QK_NOPE].reshape(KV_LORA, -1).astype(BF16)
    w_uvt = ukv[:, :, QK_NOPE:].reshape(KV_LORA, -1).T.astype(BF16)

    qt, k, vt, u = _inproj(x, mod, ln1_g, w_in_r, q_norm_g, w_uqt, kv_norm_g, w_uk, w_uvt, cos_t, sin_t, tm)
    a = _attention(qt, k, vt)

    y = _ungroup(_ssm(_regroup(u), *ssm, bsz), bsz)

    x1 = _mixout(y, a, x, mod, w_glu.astype(BF16), b_glu, attn_out_g, ssm_out_g, w_out.astype(BF16), tm)

    d_ff = w_down.shape[0]
    nch = d_ff // FF_CHUNK
    w_up_r = jnp.concatenate([w_up[:, :d_ff].reshape(d, nch, FF_CHUNK),
                              w_up[:, d_ff:].reshape(d, nch, FF_CHUNK)], axis=2).reshape(d, 2 * d_ff).astype(BF16)
    return _ffn(x1, mod, ln2_g, w_up_r, conv_w, conv_b.reshape(1, d_ff), w_down.astype(BF16), final_g, tm, last)


def kernel(x, c, positions, w_mod, b_mod, ln1_g, w_in, q_norm_g, w_uq, kv_norm_g, w_ukv, ssm_lam_re, ssm_lam_im, ssm_log_dt, ssm_b_re, ssm_b_im, ssm_c_re, ssm_c_im, ssm_d, w_glu, b_glu, attn_out_g, ssm_out_g, w_out, ln2_g, w_up, conv_w, conv_b, w_down, final_g):
    bsz, seq, d = x.shape
    cos_t, sin_t = _rope_tables_t(positions)
    depth = w_in.shape[0]
    for l in range(depth):
        mod = _mod(c, w_mod[l], b_mod[l]).reshape(bsz, 6, d)
        ssm = _ssm_operators(ssm_lam_re[l], ssm_lam_im[l], ssm_log_dt[l], ssm_b_re[l], ssm_b_im[l],
                             ssm_c_re[l], ssm_c_im[l], ssm_d[l])
        x = _layer(x, mod, cos_t, sin_t, w_in[l], ln1_g[l], q_norm_g[l], w_uq[l], kv_norm_g[l], w_ukv[l], ssm,
                   w_glu[l], b_glu[l], attn_out_g[l], ssm_out_g[l], w_out[l], ln2_g[l], w_up[l], conv_w[l],
                   conv_b[l], w_down[l], final_g, l == depth - 1)
    return x
```

```python
import functools
import math

import jax
import jax.numpy as jnp
from jax import lax
from jax.experimental import pallas as pl
from jax.experimental.pallas import tpu as pltpu

N_HEADS = 4
QK_NOPE = 128
QK_ROPE = 64
V_HEAD = 128
Q_LORA = 384
KV_LORA = 256
ROPE_THETA = 10000.0
SSM_GROUP = 16
SSM_STATE = 64
CONV_W = 3
EPS = 1e-6

HEAD_PAD = 256
V_PAD = V_HEAD + 16
ATT_BLK = 256
ATT_CHAINS = 8
SSM_CHUNK = 16
SSM_TOK = 2048
FF_CHUNK = 256
FF_GROUP = 6
MIX_SPLIT = 2
HALO = 8
VMEM_LIMIT = 56 * 1024 * 1024

F32 = jnp.float32
BF16 = jnp.bfloat16


def _rms(x, g):
    return x * lax.rsqrt(jnp.mean(x * x, axis=-1, keepdims=True) + EPS) * g


def _gelu(x):
    return 0.5 * x * (1.0 + jnp.tanh(math.sqrt(2.0 / math.pi) * (x + 0.044715 * (x * x * x))))


def _gelu_sigmoid(x):
    a = -2.0 * math.sqrt(2.0 / math.pi) * math.log2(math.e)
    return x * (1.0 / (1.0 + jnp.exp2(x * (a + (a * 0.044715) * (x * x)))))


def _sigmoid(x):
    return 1.0 / (1.0 + jnp.exp(-x))


def _dot(a, b):
    return jnp.dot(a, b, preferred_element_type=F32)


def _params(*sem, flags=None):
    return pltpu.CompilerParams(dimension_semantics=sem, vmem_limit_bytes=VMEM_LIMIT, flags=flags)


def _mod_kernel(c_ref, w_ref, b_ref, o_ref):
    c = c_ref[...]
    cond = c * _sigmoid(c)
    o_ref[...] = jnp.dot(cond, w_ref[...], preferred_element_type=F32,
                         precision=lax.Precision.HIGHEST) + b_ref[...]


def _mod(c, w_mod, b_mod):
    bsz, d = c.shape
    n = w_mod.shape[1]
    tn = 1024
    return pl.pallas_call(
        _mod_kernel,
        grid=(n // tn,),
        in_specs=[pl.BlockSpec((bsz, d), lambda j: (0, 0)),
                  pl.BlockSpec((d, tn), lambda j: (0, j)),
                  pl.BlockSpec((1, tn), lambda j: (0, j))],
        out_specs=pl.BlockSpec((bsz, tn), lambda j: (0, j)),
        out_shape=jax.ShapeDtypeStruct((bsz, n), F32),
        compiler_params=_params("arbitrary"),
        name="mod",
    )(c, w_mod, b_mod.reshape(1, n))


_NT = (((1,), (1,)), ((), ()))


def _inproj_kernel(x_ref, mod_ref, ln_ref, win_ref, qg_ref, wuqt_ref, kvg_ref, wuk_ref, wuvt_ref,
                   pos_ref, freq_ref, qt_ref, k_ref, vt_ref, u_ref, *, scale, tm):
    x = x_ref[0]
    h = _rms(x, ln_ref[...]) * (1.0 + mod_ref[0, 1:2, :]) + mod_ref[0, 0:1, :]
    z = _dot(h.astype(BF16), win_ref[...])
    o_kv = Q_LORA
    o_u = o_kv + KV_LORA
    o_kr = z.shape[1] - 128
    for slab in range(u_ref.shape[1]):
        u_ref[0, slab] = z[:, o_u + 128 * slab:o_u + 128 * (slab + 1)]
    zqn = _rms(z[:, :o_kv], qg_ref[...]).astype(BF16)
    zkvn = _rms(z[:, o_kv:o_u], kvg_ref[...]).astype(BF16)
    qt = lax.dot_general(wuqt_ref[...], zqn, _NT, preferred_element_type=F32)
    vt = lax.dot_general(wuvt_ref[...], zkvn, _NT, preferred_element_type=F32)
    kn = _dot(zkvn, wuk_ref[...])
    ang = freq_ref[...] * pos_ref[0]
    cos_t = jnp.cos(ang)
    sin_t = jnp.sin(ang)
    half = QK_ROPE // 2
    zero_t = jnp.zeros((128 - QK_ROPE, tm), F32)
    cc = jnp.concatenate([cos_t, cos_t, zero_t], axis=0).T
    ss = jnp.concatenate([-sin_t, sin_t, zero_t], axis=0).T
    zkr = z[:, o_kr:]
    kr = (zkr * cc + pltpu.roll(zkr, half, 1) * ss).astype(BF16)
    ones_rows = (lax.broadcasted_iota(jnp.int32, (V_PAD - V_HEAD, tm), 0) == 0).astype(F32)
    for hd in range(N_HEADS):
        base = HEAD_PAD * hd
        r1 = qt[base + QK_NOPE:base + QK_NOPE + half]
        r2 = qt[base + QK_NOPE + half:base + QK_NOPE + QK_ROPE]
        head = jnp.concatenate([qt[base:base + QK_NOPE], r1 * cos_t - r2 * sin_t, r2 * cos_t + r1 * sin_t,
                                qt[base + QK_NOPE + QK_ROPE:base + HEAD_PAD]], axis=0)
        head = (head * scale).astype(BF16)
        vth = jnp.concatenate([vt[V_HEAD * hd:V_HEAD * (hd + 1)], ones_rows], axis=0).astype(BF16)
        for jj in range(tm // ATT_BLK):
            qt_ref[0, hd, jj] = head[:, ATT_BLK * jj:ATT_BLK * (jj + 1)]
            vt_ref[0, hd, jj] = vth[:, ATT_BLK * jj:ATT_BLK * (jj + 1)]
        k_ref[0, hd, :, :QK_NOPE] = kn[:, QK_NOPE * hd:QK_NOPE * (hd + 1)].astype(BF16)
        k_ref[0, hd, :, QK_NOPE:] = kr


def _inproj(x, mod, ln1_g, w_in_r, q_norm_g, w_uqt, kv_norm_g, w_uk, w_uvt, positions, tm):
    bsz, seq, d = x.shape
    d_ssm = w_in_r.shape[1] - Q_LORA - KV_LORA - 128
    half = QK_ROPE // 2
    inv_freq = ROPE_THETA ** (-jnp.arange(0, QK_ROPE, 2, dtype=F32) / QK_ROPE)
    nb = tm // ATT_BLK
    const = lambda shape: pl.BlockSpec(shape, lambda b, i: (0,) * len(shape))
    scale = (QK_NOPE + QK_ROPE) ** -0.5 * math.log2(math.e)
    return pl.pallas_call(
        functools.partial(_inproj_kernel, scale=scale, tm=tm),
        grid=(bsz, seq // tm),
        in_specs=[pl.BlockSpec((1, tm, d), lambda b, i: (b, i, 0)),
                  pl.BlockSpec((1, 6, d), lambda b, i: (b, 0, 0)),
                  const((1, d)), const(w_in_r.shape), const((1, Q_LORA)), const(w_uqt.shape),
                  const((1, KV_LORA)), const(w_uk.shape), const(w_uvt.shape),
                  pl.BlockSpec((1, 1, tm), lambda b, i: (b, 0, i)), const((half, 1))],
        out_specs=[pl.BlockSpec((1, N_HEADS, nb, HEAD_PAD, ATT_BLK), lambda b, i: (b, 0, i, 0, 0)),
                   pl.BlockSpec((1, N_HEADS, tm, HEAD_PAD), lambda b, i: (b, 0, i, 0)),
                   pl.BlockSpec((1, N_HEADS, nb, V_PAD, ATT_BLK), lambda b, i: (b, 0, i, 0, 0)),
                   pl.BlockSpec((1, d_ssm // 128, tm, 128), lambda b, i: (b, 0, i, 0))],
        out_shape=[jax.ShapeDtypeStruct((bsz, N_HEADS, seq // ATT_BLK, HEAD_PAD, ATT_BLK), BF16),
                   jax.ShapeDtypeStruct((bsz, N_HEADS, seq, HEAD_PAD), BF16),
                   jax.ShapeDtypeStruct((bsz, N_HEADS, seq // ATT_BLK, V_PAD, ATT_BLK), BF16),
                   jax.ShapeDtypeStruct((bsz, d_ssm // 128, seq, 128), F32)],
        compiler_params=_params("arbitrary", "arbitrary"),
        name="inproj",
    )(x, mod, ln1_g.reshape(1, d), w_in_r, q_norm_g.reshape(1, -1), w_uqt,
      kv_norm_g.reshape(1, -1), w_uk, w_uvt, positions.astype(F32).reshape(bsz, 1, seq), inv_freq.reshape(half, 1))


def _attn_kernel(qt_ref, k_ref, vt_ref, o_ref, m_sc, acc_sc, s_sc, *, blk, nq, dv):
    qi = pl.program_id(2)
    m_sc[...] = jnp.full(m_sc.shape, -jnp.inf, F32)
    acc_sc[...] = jnp.zeros(acc_sc.shape, F32)

    def scores(j, chains):
        k = k_ref[0, 0, pl.ds(pl.multiple_of(j * blk, blk), blk), :]
        return [_dot(k, qt_ref[0, 0, c]) for c in chains]

    def softmax_values(j, chains, s_list, diag):
        vt = vt_ref[0, 0, j]
        probs = []
        for c, s in zip(chains, s_list):
            if c == diag:
                key = lax.broadcasted_iota(jnp.int32, s.shape, 0)
                qry = lax.broadcasted_iota(jnp.int32, s.shape, 1)
                s = jnp.where(key <= qry, s, -1e30)
            m = m_sc[c]
            m_new = jnp.maximum(m, jnp.max(s, axis=0, keepdims=True))
            alpha = jnp.exp2(m - m_new)
            p = jnp.exp2(s - m_new)
            m_sc[c] = m_new
            probs.append((alpha, p.astype(BF16)))
        for c, (alpha, p) in zip(chains, probs):
            acc_sc[c] = alpha * acc_sc[c] + _dot(vt, p)

    every = range(nq)
    nfull = qi * nq

    def step(j, slot):
        for c, s in zip(every, scores(j + 1, every)):
            s_sc[1 - slot, c] = s
        softmax_values(j, every, [s_sc[slot, c] for c in every], None)

    def body(i, carry):
        step(2 * i, 0)
        step(2 * i + 1, 1)
        return carry

    for c, s in zip(every, scores(0, every)):
        s_sc[0, c] = s
    lax.fori_loop(0, nfull // 2, body, 0)
    s_cur = [s_sc[0, c] for c in every]
    for kc in range(nq):
        s_next = scores(nfull + kc + 1, range(kc + 1, nq)) if kc + 1 < nq else None
        softmax_values(nfull + kc, range(kc, nq), s_cur, kc)
        s_cur = s_next
    for c in range(nq):
        acc = acc_sc[c]
        o_ref[0, blk * c:blk * (c + 1), :] = (acc[:dv] / acc[dv:dv + 1]).T.astype(o_ref.dtype)


def _attention(qt, k, vt):
    bsz, nh, nblk, dh, blk = qt.shape
    dvp = vt.shape[3]
    dv = V_HEAD
    seq = nblk * blk
    nq = min(ATT_CHAINS, nblk)
    assert nq % 2 == 0 and nblk % nq == 0
    return pl.pallas_call(
        functools.partial(_attn_kernel, blk=blk, nq=nq, dv=dv),
        grid=(bsz, nh, nblk // nq),
        in_specs=[pl.BlockSpec((1, 1, nq, dh, blk), lambda b, h, i: (b, h, i, 0, 0)),
                  pl.BlockSpec((1, 1, seq, dh), lambda b, h, i: (b, h, 0, 0)),
                  pl.BlockSpec((1, 1, nblk, dvp, blk), lambda b, h, i: (b, h, 0, 0, 0))],
        out_specs=pl.BlockSpec((1, nq * blk, dv), lambda b, h, i: (b, i, h)),
        out_shape=jax.ShapeDtypeStruct((bsz, seq, nh * dv), BF16),
        scratch_shapes=[pltpu.VMEM((nq, 1, blk), F32), pltpu.VMEM((nq, dvp, blk), F32),
                        pltpu.VMEM((2, nq, blk, blk), F32)],
        compiler_params=_params("arbitrary", "arbitrary", "arbitrary"),
        name="attention",
    )(qt, k, vt)


def _slab_rows(slab, g8, s):
    g = 8 * slab + g8
    return g // 2, (g % 2) * SSM_CHUNK * SSM_GROUP + SSM_GROUP * s


def _regroup_kernel(u_ref, z_ref):
    nslab = u_ref.shape[1]
    nc = u_ref.shape[2] // SSM_CHUNK
    for slab in range(nslab):
        for s in range(SSM_CHUNK):
            t = u_ref[0, slab, pl.ds(s, nc, stride=SSM_CHUNK), :].T.astype(BF16)
            for g8 in range(128 // SSM_GROUP):
                pair, row = _slab_rows(slab, g8, s)
                z_ref[pair, row:row + SSM_GROUP, :] = t[SSM_GROUP * g8:SSM_GROUP * (g8 + 1), :]


def _regroup(u):
    bsz, nslab, seq, _ = u.shape
    nck = seq // SSM_CHUNK
    npair = nslab * 128 // SSM_GROUP // 2
    width = 2 * SSM_CHUNK * SSM_GROUP
    nt = seq // SSM_TOK
    return pl.pallas_call(
        _regroup_kernel,
        grid=(bsz, nt),
        in_specs=[pl.BlockSpec((1, nslab, SSM_TOK, 128), lambda b, i: (b, 0, i, 0))],
        out_specs=pl.BlockSpec((npair, width, SSM_TOK // SSM_CHUNK), lambda b, i: (0, 0, b * nt + i)),
        out_shape=jax.ShapeDtypeStruct((npair, width, bsz * nck), BF16),
        compiler_params=_params("arbitrary", "arbitrary"),
        name="regroup",
    )(u)


def _ungroup_kernel(yt_ref, y_ref):
    nslab = y_ref.shape[1]
    nc = y_ref.shape[2] // SSM_CHUNK
    for slab in range(nslab):
        for s in range(SSM_CHUNK):
            pieces = []
            for g8 in range(128 // SSM_GROUP):
                pair, row = _slab_rows(slab, g8, s)
                pieces.append(yt_ref[pair, row:row + SSM_GROUP, :])
            y_ref[0, slab, pl.ds(s, nc, stride=SSM_CHUNK), :] = jnp.concatenate(pieces, axis=0).astype(F32).T


def _ungroup(yt, bsz):
    npair, width, ncol = yt.shape
    seq = ncol // bsz * SSM_CHUNK
    nslab = npair * 2 * SSM_GROUP // 128
    nt = seq // SSM_TOK
    return pl.pallas_call(
        _ungroup_kernel,
        grid=(bsz, nt),
        in_specs=[pl.BlockSpec((npair, width, SSM_TOK // SSM_CHUNK), lambda b, i: (0, 0, b * nt + i))],
        out_specs=pl.BlockSpec((1, nslab, SSM_TOK, 128), lambda b, i: (b, 0, i, 0)),
        out_shape=jax.ShapeDtypeStruct((bsz, nslab, seq, 128), F32),
        compiler_params=_params("arbitrary", "arbitrary"),
        name="ungroup",
    )(yt)


def _ssm_kernel(z_ref, toep_ref, r_ref, ore_ref, oim_ref, are_ref, aim_ref, y_ref, rt_sc, xp_sc,
                *, bsz, nck):
    ncol = bsz * nck
    cb = min(512, ncol)
    ns = are_ref.shape[-1]
    for c0 in range(0, ncol, cb):
        r = _dot(r_ref[0], z_ref[0, :, c0:c0 + cb])
        for q in range(0, cb, 128):
            rt_sc[0, c0 + q:c0 + q + 128, :] = r[:ns, q:q + 128].T
            rt_sc[1, c0 + q:c0 + q + 128, :] = r[ns:, q:q + 128].T
    a_re = jnp.broadcast_to(are_ref[0], (bsz, ns))
    a_im = jnp.broadcast_to(aim_ref[0], (bsz, ns))

    def step(c, carry):
        s_re, s_im = carry
        rows = pl.ds(c, bsz, stride=nck)
        xp_sc[0, rows, :] = s_re
        xp_sc[1, rows, :] = s_im
        n_re = a_re * s_re - a_im * s_im + rt_sc[0, rows, :]
        n_im = a_re * s_im + a_im * s_re + rt_sc[1, rows, :]
        return n_re, n_im

    zero = jnp.zeros((bsz, ns), F32)
    lax.fori_loop(0, nck, step, (zero, zero), unroll=4)
    for c0 in range(0, ncol, cb):
        y = _dot(toep_ref[0], z_ref[0, :, c0:c0 + cb])
        y += lax.dot_general(ore_ref[0], xp_sc[0, c0:c0 + cb, :].astype(BF16), _NT, preferred_element_type=F32)
        y += lax.dot_general(oim_ref[0], xp_sc[1, c0:c0 + cb, :].astype(BF16), _NT, preferred_element_type=F32)
        y_ref[0, :, c0:c0 + cb] = y.astype(y_ref.dtype)


def _ssm(z, toep, r, o_re, o_im, a_re, a_im, bsz):
    npair, width, ncol = z.shape
    ns = a_re.shape[-1]
    blk = lambda a: pl.BlockSpec((1,) + a.shape[1:], lambda g: (g,) + (0,) * (a.ndim - 1))
    return pl.pallas_call(
        functools.partial(_ssm_kernel, bsz=bsz, nck=ncol // bsz),
        grid=(npair,),
        in_specs=[blk(z), blk(toep), blk(r), blk(o_re), blk(o_im), blk(a_re), blk(a_im)],
        out_specs=pl.BlockSpec((1, width, ncol), lambda g: (g, 0, 0)),
        out_shape=jax.ShapeDtypeStruct((npair, width, ncol), BF16),
        scratch_shapes=[pltpu.VMEM((2, ncol, ns), F32)] * 2,
        compiler_params=_params("arbitrary"),
        name="ssm",
    )(z, toep, r, o_re, o_im, a_re, a_im)


def _ssm_operators(lam_re, lam_im, log_dt, b_re, b_im, c_re, c_im, d_skip):
    t_len = SSM_CHUNK
    g, p = lam_re.shape
    hch = b_re.shape[-1]
    lr = jnp.minimum(lam_re.astype(F32), -1e-4)
    li = lam_im.astype(F32)
    dt = jnp.exp(log_dt.astype(F32))[:, None]
    mag = jnp.exp(lr * dt)
    ab_re = mag * jnp.cos(li * dt)
    ab_im = mag * jnp.sin(li * dt)
    den = lr * lr + li * li
    nr, ni = ab_re - 1.0, ab_im
    z_re = ((nr * lr + ni * li) / den)[..., None]
    z_im = ((ni * lr - nr * li) / den)[..., None]
    br, bi = b_re.astype(F32), b_im.astype(F32)
    bb_re = z_re * br - z_im * bi
    bb_im = z_re * bi + z_im * br
    tau = jnp.arange(t_len + 1, dtype=F32)[None, :, None]
    pm = jnp.exp(tau * (lr * dt)[:, None, :])
    pw_re = pm * jnp.cos(tau * (li * dt)[:, None, :])
    pw_im = pm * jnp.sin(tau * (li * dt)[:, None, :])
    cr, ci = c_re.astype(F32)[:, None], c_im.astype(F32)[:, None]
    ca_re = cr * pw_re[:, :, None, :] - ci * pw_im[:, :, None, :]
    ca_im = cr * pw_im[:, :, None, :] + ci * pw_re[:, :, None, :]
    hi = lax.Precision.HIGHEST
    kern = jnp.einsum('gtop,gph->goth', jnp.concatenate([ca_re[:, :t_len], -ca_im[:, :t_len]], axis=-1),
                      jnp.concatenate([bb_re, bb_im], axis=1), precision=hi)
    kern = kern.at[:, :, 0, :].add(jnp.eye(hch, dtype=F32)[None] * d_skip.astype(F32)[:, :, None])
    krow = jnp.concatenate([kern[:, :, ::-1, :].reshape(g, hch, t_len * hch),
                            jnp.zeros((g, hch, (t_len - 1) * hch), F32)], axis=-1)
    toep = jnp.stack([krow[:, :, (t_len - 1 - t) * hch:(2 * t_len - 1 - t) * hch] for t in range(t_len)], axis=1)
    toep = toep.reshape(g, t_len * hch, t_len * hch)
    rp_re = pw_re[:, :t_len][:, ::-1].transpose(0, 2, 1)[..., None]
    rp_im = pw_im[:, :t_len][:, ::-1].transpose(0, 2, 1)[..., None]
    rr = (rp_re * bb_re[:, :, None, :] - rp_im * bb_im[:, :, None, :]).reshape(g, p, t_len * hch)
    ri = (rp_re * bb_im[:, :, None, :] + rp_im * bb_re[:, :, None, :]).reshape(g, p, t_len * hch)
    orr = ca_re[:, 1:].reshape(g, t_len * hch, p)
    oii = (-ca_im[:, 1:]).reshape(g, t_len * hch, p)

    def pair_diag(m):
        m = m.reshape(g // 2, 2, m.shape[1], m.shape[2])
        z = jnp.zeros_like(m[:, 0])
        top = jnp.concatenate([m[:, 0], z], axis=2)
        bot = jnp.concatenate([z, m[:, 1]], axis=2)
        return jnp.concatenate([top, bot], axis=1).astype(BF16)

    a_t_re = pw_re[:, t_len].reshape(g // 2, 1, 2 * p)
    a_t_im = pw_im[:, t_len].reshape(g // 2, 1, 2 * p)
    r_t = jnp.concatenate([pair_diag(rr), pair_diag(ri)], axis=1)
    return pair_diag(toep), r_t, pair_diag(orr), pair_diag(oii), a_t_re, a_t_im


def _mixer_output(y_ref, a_ref, x_ref, mod_ref, wglu_ref, bglu_ref, ag_ref, sg_ref, wout_ref):
    d_ssm = y_ref.shape[1] * y_ref.shape[3]
    d_attn = a_ref.shape[-1]
    tm = x_ref.shape[1]
    parts = [slice(r, r + tm // MIX_SPLIT) for r in range(0, tm, tm // MIX_SPLIT)]
    gls = []
    for rows in parts:
        y = jnp.concatenate([y_ref[0, slab, rows, :] for slab in range(y_ref.shape[1])], axis=-1)
        gls.append(_dot(_gelu_sigmoid(y).astype(BF16), wglu_ref[...]) + bglu_ref[...])
    outs = []
    for rows, gl in zip(parts, gls):
        s = gl[:, :d_ssm] * _sigmoid(gl[:, d_ssm:])
        sn = _rms(s, sg_ref[...]).astype(BF16)
        an = _rms(a_ref[0, rows, :].astype(F32), ag_ref[...]).astype(BF16)
        m = _dot(an, wout_ref[:d_attn, :]) + _dot(sn, wout_ref[d_attn:, :])
        outs.append(x_ref[0, rows, :] + mod_ref[0, 2:3, :] * m)
    return jnp.concatenate(outs, axis=0)


def _channel_mixer(x, halo, keep, mod_ref, ln_ref, wup_ref, cw_ref, cb_ref, wdn_ref, fg_ref, *, n_chunks, last):
    xe = jnp.concatenate([halo, x], axis=0)
    he = (_rms(xe, ln_ref[...]) * (1.0 + mod_ref[0, 4:5, :]) + mod_ref[0, 3:4, :]).astype(BF16)

    wup = lambda j: wup_ref[:, 2 * FF_CHUNK * j:2 * FF_CHUNK * (j + 1)]
    up = _dot(he, wup(0))
    acc = None
    acts = []
    for j in range(n_chunks):
        up_next = _dot(he, wup(j + 1)) if j + 1 < n_chunks else None
        gate = jnp.concatenate([up[:HALO, :FF_CHUNK] * keep, up[HALO:, :FF_CHUNK]], axis=0)
        cols = slice(FF_CHUNK * j, FF_CHUNK * (j + 1))
        conv = cw_ref[0:1, cols] * gate
        for t in range(1, CONV_W):
            conv = cw_ref[t:t + 1, cols] * gate + pltpu.roll(conv, 1, 0)
        conv = conv[HALO:] + cb_ref[:, cols]
        acts.append((_gelu_sigmoid(conv) * up[HALO:, FF_CHUNK:]).astype(BF16))
        if len(acts) == FF_GROUP or j + 1 == n_chunks:
            rows = slice(FF_CHUNK * (j + 1 - len(acts)), FF_CHUNK * (j + 1))
            down = _dot(jnp.concatenate(acts, axis=-1), wdn_ref[rows, :])
            acc = down if acc is None else acc + down
            acts = []
        up = up_next
    x2 = x + mod_ref[0, 5:6, :] * acc
    return _rms(x2, fg_ref[...]) if last else x2


def _mixffn_kernel(y_ref, a_ref, x_ref, mod_ref, wglu_ref, bglu_ref, ag_ref, sg_ref, wout_ref,
                   ln_ref, wup_ref, cw_ref, cb_ref, wdn_ref, fg_ref, o_ref, halo_sc, *, n_chunks, last):
    first = pl.program_id(1) == 0

    @pl.when(first)
    def _():
        halo_sc[...] = jnp.zeros(halo_sc.shape, F32)

    x1 = _mixer_output(y_ref, a_ref, x_ref, mod_ref, wglu_ref, bglu_ref, ag_ref, sg_ref, wout_ref)
    halo = halo_sc[...]
    halo_sc[...] = x1[x1.shape[0] - HALO:]
    keep = jnp.where(first, 0.0, 1.0)
    o_ref[0] = _channel_mixer(x1, halo, keep, mod_ref, ln_ref, wup_ref, cw_ref, cb_ref, wdn_ref, fg_ref,
                              n_chunks=n_chunks, last=last)


def _mixffn(y, a, x, mod, w_glu, b_glu, attn_g, ssm_g, w_out, ln2_g, w_up_r, conv_w, conv_b, w_down, final_g,
            tm, last):
    bsz, seq, d = x.shape
    nslab = y.shape[1]
    d_ssm, d_attn = nslab * y.shape[3], a.shape[-1]
    n_chunks = w_up_r.shape[1] // (2 * FF_CHUNK)
    const = lambda shape: pl.BlockSpec(shape, lambda b, i: (0,) * len(shape))
    weight = lambda w: pl.BlockSpec(w.shape, lambda b, i: (0,) * w.ndim, pipeline_mode=pl.Buffered(1))
    tile = lambda w: pl.BlockSpec((1, tm, w), lambda b, i: (b, i, 0))
    return pl.pallas_call(
        functools.partial(_mixffn_kernel, n_chunks=n_chunks, last=last),
        grid=(bsz, seq // tm),
        in_specs=[pl.BlockSpec((1, nslab, tm, 128), lambda b, i: (b, 0, i, 0)), tile(d_attn), tile(d),
                  pl.BlockSpec((1, 6, d), lambda b, i: (b, 0, 0)),
                  weight(w_glu), const((1, 2 * d_ssm)), const((1, d_attn)), const((1, d_ssm)), weight(w_out),
                  const((1, d)), weight(w_up_r), const(conv_w.shape), const(conv_b.shape), weight(w_down),
                  const((1, d))],
        out_specs=tile(d),
        out_shape=jax.ShapeDtypeStruct((bsz, seq, d), F32),
        scratch_shapes=[pltpu.VMEM((HALO, d), F32)],
        compiler_params=_params("arbitrary", "arbitrary"),
        name="mixffn",
    )(y, a, x, mod, w_glu, b_glu.reshape(1, -1), attn_g.reshape(1, -1), ssm_g.reshape(1, -1), w_out,
      ln2_g.reshape(1, d), w_up_r, conv_w, conv_b, w_down, final_g.reshape(1, d))


def _layer(x, mod, positions, w_in, ln1_g, q_norm_g, w_uq, kv_norm_g, w_ukv, ssm, w_glu, b_glu,
           attn_out_g, ssm_out_g, w_out, ln2_g, w_up, conv_w, conv_b, w_down, final_g, last):
    bsz, seq, d = x.shape
    tm = min(512, seq)
    half = QK_ROPE // 2
    o_kr = Q_LORA + KV_LORA
    o_u = o_kr + QK_ROPE
    d_ssm = w_in.shape[1] - o_u
    kr_w = w_in[:, o_kr:o_u]
    w_in_r = jnp.concatenate([w_in[:, :o_kr], w_in[:, o_u:], kr_w, kr_w], axis=1).astype(BF16)
    uq = w_uq.reshape(Q_LORA, N_HEADS, QK_NOPE + QK_ROPE)
    uq = jnp.pad(uq, ((0, 0), (0, 0), (0, HEAD_PAD - QK_NOPE - QK_ROPE)))
    w_uqt = uq.reshape(Q_LORA, N_HEADS * HEAD_PAD).T.astype(BF16)
    ukv = w_ukv.reshape(KV_LORA, N_HEADS, QK_NOPE + V_HEAD)
    w_uk = ukv[:, :, :QK_NOPE].reshape(KV_LORA, -1).astype(BF16)
    w_uvt = ukv[:, :, QK_NOPE:].reshape(KV_LORA, -1).T.astype(BF16)

    qt, k, vt, u = _inproj(x, mod, ln1_g, w_in_r, q_norm_g, w_uqt, kv_norm_g, w_uk, w_uvt, positions, tm)
    a = _attention(qt, k, vt)

    y = _ungroup(_ssm(_regroup(u), *ssm, bsz), bsz)

    d_ff = w_down.shape[0]
    nch = d_ff // FF_CHUNK
    w_up_r = jnp.concatenate([w_up[:, :d_ff].reshape(d, nch, FF_CHUNK),
                              w_up[:, d_ff:].reshape(d, nch, FF_CHUNK)], axis=2).reshape(d, 2 * d_ff).astype(BF16)
    return _mixffn(y, a, x, mod, w_glu.astype(BF16), b_glu, attn_out_g, ssm_out_g, w_out.astype(BF16), ln2_g,
                   w_up_r, conv_w, conv_b.reshape(1, d_ff), w_down.astype(BF16), final_g, tm, last)


def kernel(x, c, positions, w_mod, b_mod, ln1_g, w_in, q_norm_g, w_uq, kv_norm_g, w_ukv, ssm_lam_re, ssm_lam_im, ssm_log_dt, ssm_b_re, ssm_b_im, ssm_c_re, ssm_c_im, ssm_d, w_glu, b_glu, attn_out_g, ssm_out_g, w_out, ln2_g, w_up, conv_w, conv_b, w_down, final_g):
    bsz, seq, d = x.shape
    depth = w_in.shape[0]
    for l in range(depth):
        mod = _mod(c, w_mod[l], b_mod[l]).reshape(bsz, 6, d)
        ssm = _ssm_operators(ssm_lam_re[l], ssm_lam_im[l], ssm_log_dt[l], ssm_b_re[l], ssm_b_im[l],
                             ssm_c_re[l], ssm_c_im[l], ssm_d[l])
        x = _layer(x, mod, positions, w_in[l], ln1_g[l], q_norm_g[l], w_uq[l], kv_norm_g[l], w_ukv[l], ssm,
                   w_glu[l], b_glu[l], attn_out_g[l], ssm_out_g[l], w_out[l], ln2_g[l], w_up[l], conv_w[l],
                   conv_b[l], w_down[l], final_g, l == depth - 1)
    return x
```

```python
import functools
import math

import jax
import jax.numpy as jnp
from jax import lax
from jax.experimental import pallas as pl
from jax.experimental.pallas import tpu as pltpu

N_HEADS = 4
QK_NOPE = 128
QK_ROPE = 64
V_HEAD = 128
Q_LORA = 384
KV_LORA = 256
ROPE_THETA = 10000.0
SSM_GROUP = 16
SSM_STATE = 64
CONV_W = 3
EPS = 1e-6

HEAD_PAD = 256
V_PAD = V_HEAD + 16
ATT_BLK = 256
ATT_CHAINS = 8
SSM_CHUNK = 16
SSM_TOK = 2048
FF_CHUNK = 256
FF_GROUP = 6
MIX_SPLIT = 2
HALO = 8
VMEM_LIMIT = 56 * 1024 * 1024

F32 = jnp.float32
BF16 = jnp.bfloat16


def _rms(x, g):
    return x * lax.rsqrt(jnp.mean(x * x, axis=-1, keepdims=True) + EPS) * g


def _gelu(x):
    return 0.5 * x * (1.0 + jnp.tanh(math.sqrt(2.0 / math.pi) * (x + 0.044715 * (x * x * x))))


def _gelu_sigmoid(x):
    a = -2.0 * math.sqrt(2.0 / math.pi) * math.log2(math.e)
    return x * (1.0 / (1.0 + jnp.exp2(x * (a + (a * 0.044715) * (x * x)))))


def _sigmoid(x):
    return 1.0 / (1.0 + jnp.exp(-x))


def _dot(a, b):
    return jnp.dot(a, b, preferred_element_type=F32)


def _params(*sem, flags=None):
    return pltpu.CompilerParams(dimension_semantics=sem, vmem_limit_bytes=VMEM_LIMIT, flags=flags)


def _mod_kernel(c_ref, w_ref, b_ref, o_ref):
    c = c_ref[...]
    cond = c * _sigmoid(c)
    o_ref[...] = jnp.dot(cond, w_ref[...], preferred_element_type=F32,
                         precision=lax.Precision.HIGHEST) + b_ref[...]


def _mod(c, w_mod, b_mod):
    bsz, d = c.shape
    n = w_mod.shape[1]
    tn = 1024
    return pl.pallas_call(
        _mod_kernel,
        grid=(n // tn,),
        in_specs=[pl.BlockSpec((bsz, d), lambda j: (0, 0)),
                  pl.BlockSpec((d, tn), lambda j: (0, j)),
                  pl.BlockSpec((1, tn), lambda j: (0, j))],
        out_specs=pl.BlockSpec((bsz, tn), lambda j: (0, j)),
        out_shape=jax.ShapeDtypeStruct((bsz, n), F32),
        compiler_params=_params("arbitrary"),
        name="mod",
    )(c, w_mod, b_mod.reshape(1, n))


_NT = (((1,), (1,)), ((), ()))


def _inproj_kernel(x_ref, mod_ref, ln_ref, win_ref, qg_ref, wuqt_ref, kvg_ref, wuk_ref, wuvt_ref,
                   pos_ref, freq_ref, qt_ref, k_ref, vt_ref, u_ref, *, scale, tm):
    x = x_ref[0]
    h = _rms(x, ln_ref[...]) * (1.0 + mod_ref[0, 1:2, :]) + mod_ref[0, 0:1, :]
    z = _dot(h.astype(BF16), win_ref[...])
    o_kv = Q_LORA
    o_u = o_kv + KV_LORA
    o_kr = z.shape[1] - 128
    for slab in range(u_ref.shape[1]):
        u_ref[0, slab] = z[:, o_u + 128 * slab:o_u + 128 * (slab + 1)]
    zqn = _rms(z[:, :o_kv], qg_ref[...]).astype(BF16)
    zkvn = _rms(z[:, o_kv:o_u], kvg_ref[...]).astype(BF16)
    qt = lax.dot_general(wuqt_ref[...], zqn, _NT, preferred_element_type=F32)
    vt = lax.dot_general(wuvt_ref[...], zkvn, _NT, preferred_element_type=F32)
    kn = _dot(zkvn, wuk_ref[...])
    ang = freq_ref[...] * pos_ref[0]
    cos_t = jnp.cos(ang)
    sin_t = jnp.sin(ang)
    half = QK_ROPE // 2
    zero_t = jnp.zeros((128 - QK_ROPE, tm), F32)
    cc = jnp.concatenate([cos_t, cos_t, zero_t], axis=0).T
    ss = jnp.concatenate([-sin_t, sin_t, zero_t], axis=0).T
    zkr = z[:, o_kr:]
    kr = (zkr * cc + pltpu.roll(zkr, half, 1) * ss).astype(BF16)
    ones_rows = (lax.broadcasted_iota(jnp.int32, (V_PAD - V_HEAD, tm), 0) == 0).astype(F32)
    for hd in range(N_HEADS):
        base = (QK_NOPE + QK_ROPE) * hd
        r1 = qt[base + QK_NOPE:base + QK_NOPE + half]
        r2 = qt[base + QK_NOPE + half:base + QK_NOPE + QK_ROPE]
        head = jnp.concatenate([qt[base:base + QK_NOPE], r1 * cos_t - r2 * sin_t, r2 * cos_t + r1 * sin_t,
                                zero_t[:HEAD_PAD - QK_NOPE - QK_ROPE]], axis=0)
        head = (head * scale).astype(BF16)
        vth = jnp.concatenate([vt[V_HEAD * hd:V_HEAD * (hd + 1)], ones_rows], axis=0).astype(BF16)
        for jj in range(tm // ATT_BLK):
            qt_ref[0, hd, jj] = head[:, ATT_BLK * jj:ATT_BLK * (jj + 1)]
            vt_ref[0, hd, jj] = vth[:, ATT_BLK * jj:ATT_BLK * (jj + 1)]
        k_ref[0, hd, :, :QK_NOPE] = kn[:, QK_NOPE * hd:QK_NOPE * (hd + 1)].astype(BF16)
        k_ref[0, hd, :, QK_NOPE:] = kr


def _inproj(x, mod, ln1_g, w_in_r, q_norm_g, w_uqt, kv_norm_g, w_uk, w_uvt, positions, tm):
    bsz, seq, d = x.shape
    d_ssm = w_in_r.shape[1] - Q_LORA - KV_LORA - 128
    half = QK_ROPE // 2
    inv_freq = ROPE_THETA ** (-jnp.arange(0, QK_ROPE, 2, dtype=F32) / QK_ROPE)
    nb = tm // ATT_BLK
    const = lambda shape: pl.BlockSpec(shape, lambda b, i: (0,) * len(shape))
    scale = (QK_NOPE + QK_ROPE) ** -0.5 * math.log2(math.e)
    return pl.pallas_call(
        functools.partial(_inproj_kernel, scale=scale, tm=tm),
        grid=(bsz, seq // tm),
        in_specs=[pl.BlockSpec((1, tm, d), lambda b, i: (b, i, 0)),
                  pl.BlockSpec((1, 6, d), lambda b, i: (b, 0, 0)),
                  const((1, d)), const(w_in_r.shape), const((1, Q_LORA)), const(w_uqt.shape),
                  const((1, KV_LORA)), const(w_uk.shape), const(w_uvt.shape),
                  pl.BlockSpec((1, 1, tm), lambda b, i: (b, 0, i)), const((half, 1))],
        out_specs=[pl.BlockSpec((1, N_HEADS, nb, HEAD_PAD, ATT_BLK), lambda b, i: (b, 0, i, 0, 0)),
                   pl.BlockSpec((1, N_HEADS, tm, HEAD_PAD), lambda b, i: (b, 0, i, 0)),
                   pl.BlockSpec((1, N_HEADS, nb, V_PAD, ATT_BLK), lambda b, i: (b, 0, i, 0, 0)),
                   pl.BlockSpec((1, d_ssm // 128, tm, 128), lambda b, i: (b, 0, i, 0))],
        out_shape=[jax.ShapeDtypeStruct((bsz, N_HEADS, seq // ATT_BLK, HEAD_PAD, ATT_BLK), BF16),
                   jax.ShapeDtypeStruct((bsz, N_HEADS, seq, HEAD_PAD), BF16),
                   jax.ShapeDtypeStruct((bsz, N_HEADS, seq // ATT_BLK, V_PAD, ATT_BLK), BF16),
                   jax.ShapeDtypeStruct((bsz, d_ssm // 128, seq, 128), F32)],
        compiler_params=_params("arbitrary", "arbitrary"),
        name="inproj",
    )(x, mod, ln1_g.reshape(1, d), w_in_r, q_norm_g.reshape(1, -1), w_uqt,
      kv_norm_g.reshape(1, -1), w_uk, w_uvt, positions.astype(F32).reshape(bsz, 1, seq), inv_freq.reshape(half, 1))


def _attn_kernel(qt_ref, k_ref, vt_ref, o_ref, m_sc, acc_sc, s_sc, *, blk, nq, dv):
    qi = pl.program_id(2)
    m_sc[...] = jnp.full(m_sc.shape, -jnp.inf, F32)
    acc_sc[...] = jnp.zeros(acc_sc.shape, F32)

    def scores(j, chains):
        k = k_ref[0, 0, pl.ds(pl.multiple_of(j * blk, blk), blk), :]
        return [_dot(k, qt_ref[0, 0, c]) for c in chains]

    def softmax_values(j, chains, s_list, diag):
        vt = vt_ref[0, 0, j]
        probs = []
        for c, s in zip(chains, s_list):
            if c == diag:
                key = lax.broadcasted_iota(jnp.int32, s.shape, 0)
                qry = lax.broadcasted_iota(jnp.int32, s.shape, 1)
                s = jnp.where(key <= qry, s, -1e30)
            m = m_sc[c]
            m_new = jnp.maximum(m, jnp.max(s, axis=0, keepdims=True))
            alpha = jnp.exp2(m - m_new)
            p = jnp.exp2(s - m_new)
            m_sc[c] = m_new
            probs.append((alpha, p.astype(BF16)))
        for c, (alpha, p) in zip(chains, probs):
            acc_sc[c] = alpha * acc_sc[c] + _dot(vt, p)

    every = range(nq)
    nfull = qi * nq

    def step(j, slot):
        for c, s in zip(every, scores(j + 1, every)):
            s_sc[1 - slot, c] = s
        softmax_values(j, every, [s_sc[slot, c] for c in every], None)

    def body(i, carry):
        step(2 * i, 0)
        step(2 * i + 1, 1)
        return carry

    for c, s in zip(every, scores(0, every)):
        s_sc[0, c] = s
    lax.fori_loop(0, nfull // 2, body, 0)
    s_cur = [s_sc[0, c] for c in every]
    for kc in range(nq):
        s_next = scores(nfull + kc + 1, range(kc + 1, nq)) if kc + 1 < nq else None
        softmax_values(nfull + kc, range(kc, nq), s_cur, kc)
        s_cur = s_next
    for c in range(nq):
        acc = acc_sc[c]
        o_ref[0, blk * c:blk * (c + 1), :] = (acc[:dv] / acc[dv:dv + 1]).T.astype(o_ref.dtype)


def _attention(qt, k, vt):
    bsz, nh, nblk, dh, blk = qt.shape
    dvp = vt.shape[3]
    dv = V_HEAD
    seq = nblk * blk
    nq = min(ATT_CHAINS, nblk)
    assert nq % 2 == 0 and nblk % nq == 0
    return pl.pallas_call(
        functools.partial(_attn_kernel, blk=blk, nq=nq, dv=dv),
        grid=(bsz, nh, nblk // nq),
        in_specs=[pl.BlockSpec((1, 1, nq, dh, blk), lambda b, h, i: (b, h, i, 0, 0)),
                  pl.BlockSpec((1, 1, seq, dh), lambda b, h, i: (b, h, 0, 0)),
                  pl.BlockSpec((1, 1, nblk, dvp, blk), lambda b, h, i: (b, h, 0, 0, 0))],
        out_specs=pl.BlockSpec((1, nq * blk, dv), lambda b, h, i: (b, i, h)),
        out_shape=jax.ShapeDtypeStruct((bsz, seq, nh * dv), BF16),
        scratch_shapes=[pltpu.VMEM((nq, 1, blk), F32), pltpu.VMEM((nq, dvp, blk), F32),
                        pltpu.VMEM((2, nq, blk, blk), F32)],
        compiler_params=_params("arbitrary", "arbitrary", "arbitrary"),
        name="attention",
    )(qt, k, vt)


def _slab_rows(slab, g8, s):
    g = 8 * slab + g8
    return g // 2, (g % 2) * SSM_CHUNK * SSM_GROUP + SSM_GROUP * s


def _regroup_kernel(u_ref, z_ref):
    nslab = u_ref.shape[1]
    nc = u_ref.shape[2] // SSM_CHUNK
    for slab in range(nslab):
        for s in range(SSM_CHUNK):
            t = u_ref[0, slab, pl.ds(s, nc, stride=SSM_CHUNK), :].T.astype(BF16)
            for g8 in range(128 // SSM_GROUP):
                pair, row = _slab_rows(slab, g8, s)
                z_ref[pair, row:row + SSM_GROUP, :] = t[SSM_GROUP * g8:SSM_GROUP * (g8 + 1), :]


def _regroup(u):
    bsz, nslab, seq, _ = u.shape
    nck = seq // SSM_CHUNK
    npair = nslab * 128 // SSM_GROUP // 2
    width = 2 * SSM_CHUNK * SSM_GROUP
    nt = seq // SSM_TOK
    return pl.pallas_call(
        _regroup_kernel,
        grid=(bsz, nt),
        in_specs=[pl.BlockSpec((1, nslab, SSM_TOK, 128), lambda b, i: (b, 0, i, 0))],
        out_specs=pl.BlockSpec((npair, width, SSM_TOK // SSM_CHUNK), lambda b, i: (0, 0, b * nt + i)),
        out_shape=jax.ShapeDtypeStruct((npair, width, bsz * nck), BF16),
        compiler_params=_params("arbitrary", "arbitrary"),
        name="regroup",
    )(u)


def _ungroup_kernel(yt_ref, y_ref):
    nslab = y_ref.shape[1]
    nc = y_ref.shape[2] // SSM_CHUNK
    for slab in range(nslab):
        for s in range(SSM_CHUNK):
            pieces = []
            for g8 in range(128 // SSM_GROUP):
                pair, row = _slab_rows(slab, g8, s)
                pieces.append(yt_ref[pair, row:row + SSM_GROUP, :])
            y_ref[0, slab, pl.ds(s, nc, stride=SSM_CHUNK), :] = jnp.concatenate(pieces, axis=0).astype(F32).T


def _ungroup(yt, bsz):
    npair, width, ncol = yt.shape
    seq = ncol // bsz * SSM_CHUNK
    nslab = npair * 2 * SSM_GROUP // 128
    nt = seq // SSM_TOK
    return pl.pallas_call(
        _ungroup_kernel,
        grid=(bsz, nt),
        in_specs=[pl.BlockSpec((npair, width, SSM_TOK // SSM_CHUNK), lambda b, i: (0, 0, b * nt + i))],
        out_specs=pl.BlockSpec((1, nslab, SSM_TOK, 128), lambda b, i: (b, 0, i, 0)),
        out_shape=jax.ShapeDtypeStruct((bsz, nslab, seq, 128), F32),
        compiler_params=_params("arbitrary", "arbitrary"),
        name="ungroup",
    )(yt)


def _ssm_kernel(z_ref, toep_ref, r_ref, o_ref, are_ref, aim_ref, y_ref, rt_sc, xp_sc, *, bsz, nck):
    ncol = bsz * nck
    cb = min(512, ncol)
    ns = are_ref.shape[-1]
    gw = z_ref.shape[1] // 2
    zg = lambda g2, c0: z_ref[0, gw * g2:gw * (g2 + 1), c0:c0 + cb]
    for c0 in range(0, ncol, cb):
        r0 = _dot(r_ref[0], zg(0, c0))
        r1 = _dot(r_ref[1], zg(1, c0))
        r_re = jnp.concatenate([r0[:ns // 2], r1[:ns // 2]], axis=0)
        r_im = jnp.concatenate([r0[ns // 2:], r1[ns // 2:]], axis=0)
        for q in range(0, cb, 128):
            rt_sc[0, c0 + q:c0 + q + 128, :] = r_re[:, q:q + 128].T
            rt_sc[1, c0 + q:c0 + q + 128, :] = r_im[:, q:q + 128].T
    a_re = jnp.broadcast_to(are_ref[0], (bsz, ns))
    a_im = jnp.broadcast_to(aim_ref[0], (bsz, ns))

    def step(c, carry):
        s_re, s_im = carry
        rows = pl.ds(c, bsz, stride=nck)
        xp_sc[0, rows, :] = s_re
        xp_sc[1, rows, :] = s_im
        n_re = a_re * s_re - a_im * s_im + rt_sc[0, rows, :]
        n_im = a_re * s_im + a_im * s_re + rt_sc[1, rows, :]
        return n_re, n_im

    zero = jnp.zeros((bsz, ns), F32)
    lax.fori_loop(0, nck, step, (zero, zero), unroll=4)
    for c0 in range(0, ncol, cb):
        xp = jnp.concatenate([xp_sc[0, c0:c0 + cb, :], xp_sc[1, c0:c0 + cb, :]], axis=-1).astype(BF16)
        y = lax.dot_general(o_ref[0], xp, _NT, preferred_element_type=F32)
        y += jnp.concatenate([_dot(toep_ref[0], zg(0, c0)), _dot(toep_ref[1], zg(1, c0))], axis=0)
        y_ref[0, :, c0:c0 + cb] = y.astype(y_ref.dtype)


def _ssm(z, toep, r, o, a_re, a_im, bsz):
    npair, width, ncol = z.shape
    ns = a_re.shape[-1]
    blk = lambda a: pl.BlockSpec((1,) + a.shape[1:], lambda g: (g,) + (0,) * (a.ndim - 1))
    two = lambda a: pl.BlockSpec((2,) + a.shape[1:], lambda g: (g,) + (0,) * (a.ndim - 1))
    return pl.pallas_call(
        functools.partial(_ssm_kernel, bsz=bsz, nck=ncol // bsz),
        grid=(npair,),
        in_specs=[blk(z), two(toep), two(r), blk(o), blk(a_re), blk(a_im)],
        out_specs=pl.BlockSpec((1, width, ncol), lambda g: (g, 0, 0)),
        out_shape=jax.ShapeDtypeStruct((npair, width, ncol), BF16),
        scratch_shapes=[pltpu.VMEM((2, ncol, ns), F32)] * 2,
        compiler_params=_params("arbitrary"),
        name="ssm",
    )(z, toep, r, o, a_re, a_im)


def _ssm_operators(lam_re, lam_im, log_dt, b_re, b_im, c_re, c_im, d_skip):
    t_len = SSM_CHUNK
    g, p = lam_re.shape
    hch = b_re.shape[-1]
    lr = jnp.minimum(lam_re.astype(F32), -1e-4)
    li = lam_im.astype(F32)
    dt = jnp.exp(log_dt.astype(F32))[:, None]
    mag = jnp.exp(lr * dt)
    ab_re = mag * jnp.cos(li * dt)
    ab_im = mag * jnp.sin(li * dt)
    den = lr * lr + li * li
    nr, ni = ab_re - 1.0, ab_im
    z_re = ((nr * lr + ni * li) / den)[..., None]
    z_im = ((ni * lr - nr * li) / den)[..., None]
    br, bi = b_re.astype(F32), b_im.astype(F32)
    bb_re = z_re * br - z_im * bi
    bb_im = z_re * bi + z_im * br
    tau = jnp.arange(t_len + 1, dtype=F32)[None, :, None]
    pm = jnp.exp(tau * (lr * dt)[:, None, :])
    pw_re = pm * jnp.cos(tau * (li * dt)[:, None, :])
    pw_im = pm * jnp.sin(tau * (li * dt)[:, None, :])
    cr, ci = c_re.astype(F32)[:, None], c_im.astype(F32)[:, None]
    ca_re = cr * pw_re[:, :, None, :] - ci * pw_im[:, :, None, :]
    ca_im = cr * pw_im[:, :, None, :] + ci * pw_re[:, :, None, :]
    ca = jnp.concatenate([ca_re[:, :t_len], -ca_im[:, :t_len]], axis=-1).transpose(0, 2, 1, 3)
    bb = jnp.concatenate([bb_re, bb_im], axis=1).transpose(0, 2, 1)
    kern = jnp.sum(ca[:, :, :, None, :] * bb[:, None, None, :, :], axis=-1)
    kern = kern.at[:, :, 0, :].add(jnp.eye(hch, dtype=F32)[None] * d_skip.astype(F32)[:, :, None])
    krow = jnp.concatenate([kern[:, :, ::-1, :].reshape(g, hch, t_len * hch),
                            jnp.zeros((g, hch, (t_len - 1) * hch), F32)], axis=-1)
    toep = jnp.stack([krow[:, :, (t_len - 1 - t) * hch:(2 * t_len - 1 - t) * hch] for t in range(t_len)], axis=1)
    toep = toep.reshape(g, t_len * hch, t_len * hch)
    rp_re = pw_re[:, :t_len][:, ::-1].transpose(0, 2, 1)[..., None]
    rp_im = pw_im[:, :t_len][:, ::-1].transpose(0, 2, 1)[..., None]
    rr = (rp_re * bb_re[:, :, None, :] - rp_im * bb_im[:, :, None, :]).reshape(g, p, t_len * hch)
    ri = (rp_re * bb_im[:, :, None, :] + rp_im * bb_re[:, :, None, :]).reshape(g, p, t_len * hch)
    orr = ca_re[:, 1:].reshape(g, t_len * hch, p)
    oii = (-ca_im[:, 1:]).reshape(g, t_len * hch, p)

    def pair_diag(m):
        m = m.reshape(g // 2, 2, m.shape[1], m.shape[2])
        z = jnp.zeros_like(m[:, 0])
        top = jnp.concatenate([m[:, 0], z], axis=2)
        bot = jnp.concatenate([z, m[:, 1]], axis=2)
        return jnp.concatenate([top, bot], axis=1).astype(BF16)

    a_t_re = pw_re[:, t_len].reshape(g // 2, 1, 2 * p)
    a_t_im = pw_im[:, t_len].reshape(g // 2, 1, 2 * p)
    r_g = jnp.concatenate([rr, ri], axis=1).astype(BF16)
    o_pair = jnp.concatenate([pair_diag(orr), pair_diag(oii)], axis=2)
    return toep.astype(BF16), r_g, o_pair, a_t_re, a_t_im


def _mixer_output(y_ref, a_ref, x_ref, mod_ref, wglu_ref, bglu_ref, ag_ref, sg_ref, wout_ref):
    d_ssm = y_ref.shape[1] * y_ref.shape[3]
    d_attn = a_ref.shape[-1]
    tm = x_ref.shape[1]
    parts = [slice(r, r + tm // MIX_SPLIT) for r in range(0, tm, tm // MIX_SPLIT)]
    gls = []
    for rows in parts:
        y = jnp.concatenate([y_ref[0, slab, rows, :] for slab in range(y_ref.shape[1])], axis=-1)
        gls.append(_dot(_gelu_sigmoid(y).astype(BF16), wglu_ref[...]) + bglu_ref[...])
    outs = []
    for rows, gl in zip(parts, gls):
        s = gl[:, :d_ssm] * _sigmoid(gl[:, d_ssm:])
        sn = _rms(s, sg_ref[...]).astype(BF16)
        an = _rms(a_ref[0, rows, :].astype(F32), ag_ref[...]).astype(BF16)
        m = _dot(an, wout_ref[:d_attn, :]) + _dot(sn, wout_ref[d_attn:, :])
        outs.append(x_ref[0, rows, :] + mod_ref[0, 2:3, :] * m)
    return jnp.concatenate(outs, axis=0)


def _channel_mixer(x, gate_halo, mod_ref, ln_ref, wup_ref, cw_ref, cb_ref, wdn_ref, fg_ref, *, n_chunks, last):
    tm = x.shape[0]
    d_ff = wdn_ref.shape[0]
    h = (_rms(x, ln_ref[...]) * (1.0 + mod_ref[0, 4:5, :]) + mod_ref[0, 3:4, :]).astype(BF16)

    def up_proj(j):
        cols = slice(FF_CHUNK * j, FF_CHUNK * (j + 1))
        vals = slice(d_ff + FF_CHUNK * j, d_ff + FF_CHUNK * (j + 1))
        return _dot(h, wup_ref[:, cols]), _dot(h, wup_ref[:, vals])

    up = up_proj(0)
    acc = None
    acts = []
    for j in range(n_chunks):
        up_next = up_proj(j + 1) if j + 1 < n_chunks else None
        cols = slice(FF_CHUNK * j, FF_CHUNK * (j + 1))
        gate = jnp.concatenate([gate_halo[:, cols], up[0]], axis=0)
        gate_halo[:, cols] = up[0][tm - HALO:]
        conv = cw_ref[0:1, cols] * gate
        for t in range(1, CONV_W):
            conv = cw_ref[t:t + 1, cols] * gate + pltpu.roll(conv, 1, 0)
        conv = conv[HALO:] + cb_ref[:, cols]
        acts.append((_gelu_sigmoid(conv) * up[1]).astype(BF16))
        if len(acts) == FF_GROUP or j + 1 == n_chunks:
            rows = slice(FF_CHUNK * (j + 1 - len(acts)), FF_CHUNK * (j + 1))
            down = _dot(jnp.concatenate(acts, axis=-1), wdn_ref[rows, :])
            acc = down if acc is None else acc + down
            acts = []
        up = up_next
    x2 = x + mod_ref[0, 5:6, :] * acc
    return _rms(x2, fg_ref[...]) if last else x2


def _mixffn_kernel(y_ref, a_ref, x_ref, mod_ref, wglu_ref, bglu_ref, ag_ref, sg_ref, wout_ref,
                   ln_ref, wup_ref, cw_ref, cb_ref, wdn_ref, fg_ref, o_ref, gate_halo, *, n_chunks, last):
    @pl.when(pl.program_id(1) == 0)
    def _():
        gate_halo[...] = jnp.zeros(gate_halo.shape, F32)

    x1 = _mixer_output(y_ref, a_ref, x_ref, mod_ref, wglu_ref, bglu_ref, ag_ref, sg_ref, wout_ref)
    o_ref[0] = _channel_mixer(x1, gate_halo, mod_ref, ln_ref, wup_ref, cw_ref, cb_ref, wdn_ref, fg_ref,
                              n_chunks=n_chunks, last=last)


def _mixffn(y, a, x, mod, w_glu, b_glu, attn_g, ssm_g, w_out, ln2_g, w_up_r, conv_w, conv_b, w_down, final_g,
            tm, last):
    bsz, seq, d = x.shape
    nslab = y.shape[1]
    d_ssm, d_attn = nslab * y.shape[3], a.shape[-1]
    d_ff = w_down.shape[0]
    n_chunks = d_ff // FF_CHUNK
    const = lambda shape: pl.BlockSpec(shape, lambda b, i: (0,) * len(shape))
    weight = lambda w: pl.BlockSpec(w.shape, lambda b, i: (0,) * w.ndim, pipeline_mode=pl.Buffered(1))
    tile = lambda w: pl.BlockSpec((1, tm, w), lambda b, i: (b, i, 0))
    return pl.pallas_call(
        functools.partial(_mixffn_kernel, n_chunks=n_chunks, last=last),
        grid=(bsz, seq // tm),
        in_specs=[pl.BlockSpec((1, nslab, tm, 128), lambda b, i: (b, 0, i, 0)), tile(d_attn), tile(d),
                  pl.BlockSpec((1, 6, d), lambda b, i: (b, 0, 0)),
                  weight(w_glu), const((1, 2 * d_ssm)), const((1, d_attn)), const((1, d_ssm)), weight(w_out),
                  const((1, d)), weight(w_up_r), const(conv_w.shape), const(conv_b.shape), weight(w_down),
                  const((1, d))],
        out_specs=tile(d),
        out_shape=jax.ShapeDtypeStruct((bsz, seq, d), F32),
        scratch_shapes=[pltpu.VMEM((HALO, d_ff), F32)],
        compiler_params=_params("arbitrary", "arbitrary"),
        name="mixffn",
    )(y, a, x, mod, w_glu, b_glu.reshape(1, -1), attn_g.reshape(1, -1), ssm_g.reshape(1, -1), w_out,
      ln2_g.reshape(1, d), w_up_r, conv_w, conv_b, w_down, final_g.reshape(1, d))


def _layer(x, mod, positions, w_in, ln1_g, q_norm_g, w_uq, kv_norm_g, w_ukv, ssm, w_glu, b_glu,
           attn_out_g, ssm_out_g, w_out, ln2_g, w_up, conv_w, conv_b, w_down, final_g, last):
    bsz, seq, d = x.shape
    tm = min(512, seq)
    half = QK_ROPE // 2
    o_kr = Q_LORA + KV_LORA
    o_u = o_kr + QK_ROPE
    d_ssm = w_in.shape[1] - o_u
    kr_w = w_in[:, o_kr:o_u]
    w_in_r = jnp.concatenate([w_in[:, :o_kr], w_in[:, o_u:], kr_w, kr_w], axis=1).astype(BF16)
    w_uqt = w_uq.T.astype(BF16)
    ukv = w_ukv.reshape(KV_LORA, N_HEADS, QK_NOPE + V_HEAD)
    w_uk = ukv[:, :, :QK_NOPE].reshape(KV_LORA, -1).astype(BF16)
    w_uvt = ukv[:, :, QK_NOPE:].reshape(KV_LORA, -1).T.astype(BF16)

    qt, k, vt, u = _inproj(x, mod, ln1_g, w_in_r, q_norm_g, w_uqt, kv_norm_g, w_uk, w_uvt, positions, tm)
    a = _attention(qt, k, vt)

    y = _ungroup(_ssm(_regroup(u), *ssm, bsz), bsz)

    return _mixffn(y, a, x, mod, w_glu.astype(BF16), b_glu, attn_out_g, ssm_out_g, w_out.astype(BF16), ln2_g,
                   w_up.astype(BF16), conv_w, conv_b.reshape(1, -1), w_down.astype(BF16), final_g, tm, last)


def kernel(x, c, positions, w_mod, b_mod, ln1_g, w_in, q_norm_g, w_uq, kv_norm_g, w_ukv, ssm_lam_re, ssm_lam_im, ssm_log_dt, ssm_b_re, ssm_b_im, ssm_c_re, ssm_c_im, ssm_d, w_glu, b_glu, attn_out_g, ssm_out_g, w_out, ln2_g, w_up, conv_w, conv_b, w_down, final_g):
    bsz, seq, d = x.shape
    depth = w_in.shape[0]
    for l in range(depth):
        mod = _mod(c, w_mod[l], b_mod[l]).reshape(bsz, 6, d)
        ssm = _ssm_operators(ssm_lam_re[l], ssm_lam_im[l], ssm_log_dt[l], ssm_b_re[l], ssm_b_im[l],
                             ssm_c_re[l], ssm_c_im[l], ssm_d[l])
        x = _layer(x, mod, positions, w_in[l], ln1_g[l], q_norm_g[l], w_uq[l], kv_norm_g[l], w_ukv[l], ssm,
                   w_glu[l], b_glu[l], attn_out_g[l], ssm_out_g[l], w_out[l], ln2_g[l], w_up[l], conv_w[l],
                   conv_b[l], w_down[l], final_g, l == depth - 1)
    return x
```

```python
import functools
import math

import jax
import jax.numpy as jnp
from jax import lax
from jax.experimental import pallas as pl
from jax.experimental.pallas import tpu as pltpu

N_HEADS = 4
QK_NOPE = 128
QK_ROPE = 64
V_HEAD = 128
Q_LORA = 384
KV_LORA = 256
ROPE_THETA = 10000.0
SSM_GROUP = 16
SSM_STATE = 64
CONV_W = 3
EPS = 1e-6

HEAD_PAD = 256
V_PAD = V_HEAD + 16
ATT_BLK = 256
ATT_CHAINS = 8
ATT_UNROLL = 8
SSM_CHUNK = 16
SSM_TOK = 2048
FF_CHUNK = 256
FF_GROUP = 6
MIX_SPLIT = 2
HALO = 8
VMEM_LIMIT = 56 * 1024 * 1024

F32 = jnp.float32
BF16 = jnp.bfloat16


def _rms(x, g):
    return x * lax.rsqrt(jnp.mean(x * x, axis=-1, keepdims=True) + EPS) * g


def _gelu(x):
    return 0.5 * x * (1.0 + jnp.tanh(math.sqrt(2.0 / math.pi) * (x + 0.044715 * (x * x * x))))


def _gelu_sigmoid(x):
    a = -2.0 * math.sqrt(2.0 / math.pi) * math.log2(math.e)
    return x * (1.0 / (1.0 + jnp.exp2(x * (a + (a * 0.044715) * (x * x)))))


def _sigmoid(x):
    return 1.0 / (1.0 + jnp.exp(-x))


def _dot(a, b):
    return jnp.dot(a, b, preferred_element_type=F32)


def _params(*sem, flags=None):
    return pltpu.CompilerParams(dimension_semantics=sem, vmem_limit_bytes=VMEM_LIMIT, flags=flags)


def _mod_kernel(c_ref, w_ref, b_ref, o_ref):
    c = c_ref[...]
    cond = c * _sigmoid(c)
    o_ref[...] = jnp.dot(cond, w_ref[...], preferred_element_type=F32,
                         precision=lax.Precision.HIGHEST) + b_ref[...]


def _mod(c, w_mod, b_mod):
    bsz, d = c.shape
    n = w_mod.shape[1]
    tn = 1024
    return pl.pallas_call(
        _mod_kernel,
        grid=(n // tn,),
        in_specs=[pl.BlockSpec((bsz, d), lambda j: (0, 0)),
                  pl.BlockSpec((d, tn), lambda j: (0, j)),
                  pl.BlockSpec((1, tn), lambda j: (0, j))],
        out_specs=pl.BlockSpec((bsz, tn), lambda j: (0, j)),
        out_shape=jax.ShapeDtypeStruct((bsz, n), F32),
        compiler_params=_params("arbitrary"),
        name="mod",
    )(c, w_mod, b_mod.reshape(1, n))


_NT = (((1,), (1,)), ((), ()))


def _inproj_kernel(x_ref, mod_ref, ln_ref, win_ref, qg_ref, wuqt_ref, kvg_ref, wuk_ref, wuvt_ref,
                   pos_ref, freq_ref, qt_ref, k_ref, vt_ref, u_ref, *, scale, tm):
    x = x_ref[0]
    h = _rms(x, ln_ref[...]) * (1.0 + mod_ref[0, 1:2, :]) + mod_ref[0, 0:1, :]
    z = _dot(h.astype(BF16), win_ref[...])
    o_kv = Q_LORA
    o_u = o_kv + KV_LORA
    o_kr = z.shape[1] - 128
    for slab in range(u_ref.shape[1]):
        u_ref[0, slab] = z[:, o_u + 128 * slab:o_u + 128 * (slab + 1)]
    zqn = _rms(z[:, :o_kv], qg_ref[...]).astype(BF16)
    zkvn = _rms(z[:, o_kv:o_u], kvg_ref[...]).astype(BF16)
    qt = lax.dot_general(wuqt_ref[...], zqn, _NT, preferred_element_type=F32)
    vt = lax.dot_general(wuvt_ref[...], zkvn, _NT, preferred_element_type=F32)
    kn = _dot(zkvn, wuk_ref[...])
    ang = freq_ref[...] * pos_ref[0]
    cos_t = jnp.cos(ang)
    sin_t = jnp.sin(ang)
    half = QK_ROPE // 2
    zero_t = jnp.zeros((128 - QK_ROPE, tm), F32)
    cc = jnp.concatenate([cos_t, cos_t, zero_t], axis=0).T
    ss = jnp.concatenate([-sin_t, sin_t, zero_t], axis=0).T
    zkr = z[:, o_kr:]
    kr = (zkr * cc + pltpu.roll(zkr, half, 1) * ss).astype(BF16)
    ones_rows = (lax.broadcasted_iota(jnp.int32, (V_PAD - V_HEAD, tm), 0) == 0).astype(F32)
    for hd in range(N_HEADS):
        base = (QK_NOPE + QK_ROPE) * hd
        r1 = qt[base + QK_NOPE:base + QK_NOPE + half]
        r2 = qt[base + QK_NOPE + half:base + QK_NOPE + QK_ROPE]
        head = jnp.concatenate([qt[base:base + QK_NOPE], r1 * cos_t - r2 * sin_t, r2 * cos_t + r1 * sin_t,
                                zero_t[:HEAD_PAD - QK_NOPE - QK_ROPE]], axis=0)
        head = (head * scale).astype(BF16)
        vth = jnp.concatenate([vt[V_HEAD * hd:V_HEAD * (hd + 1)], ones_rows], axis=0).astype(BF16)
        for jj in range(tm // ATT_BLK):
            qt_ref[0, hd, jj] = head[:, ATT_BLK * jj:ATT_BLK * (jj + 1)]
            vt_ref[0, hd, jj] = vth[:, ATT_BLK * jj:ATT_BLK * (jj + 1)]
        k_ref[0, hd, :, :QK_NOPE] = kn[:, QK_NOPE * hd:QK_NOPE * (hd + 1)].astype(BF16)
        k_ref[0, hd, :, QK_NOPE:] = kr


def _inproj(x, mod, ln1_g, w_in_r, q_norm_g, w_uqt, kv_norm_g, w_uk, w_uvt, positions, tm):
    bsz, seq, d = x.shape
    d_ssm = w_in_r.shape[1] - Q_LORA - KV_LORA - 128
    half = QK_ROPE // 2
    inv_freq = ROPE_THETA ** (-jnp.arange(0, QK_ROPE, 2, dtype=F32) / QK_ROPE)
    nb = tm // ATT_BLK
    const = lambda shape: pl.BlockSpec(shape, lambda b, i: (0,) * len(shape))
    scale = (QK_NOPE + QK_ROPE) ** -0.5 * math.log2(math.e)
    return pl.pallas_call(
        functools.partial(_inproj_kernel, scale=scale, tm=tm),
        grid=(bsz, seq // tm),
        in_specs=[pl.BlockSpec((1, tm, d), lambda b, i: (b, i, 0)),
                  pl.BlockSpec((1, 6, d), lambda b, i: (b, 0, 0)),
                  const((1, d)), const(w_in_r.shape), const((1, Q_LORA)), const(w_uqt.shape),
                  const((1, KV_LORA)), const(w_uk.shape), const(w_uvt.shape),
                  pl.BlockSpec((1, 1, tm), lambda b, i: (b, 0, i)), const((half, 1))],
        out_specs=[pl.BlockSpec((1, N_HEADS, nb, HEAD_PAD, ATT_BLK), lambda b, i: (b, 0, i, 0, 0)),
                   pl.BlockSpec((1, N_HEADS, tm, HEAD_PAD), lambda b, i: (b, 0, i, 0)),
                   pl.BlockSpec((1, N_HEADS, nb, V_PAD, ATT_BLK), lambda b, i: (b, 0, i, 0, 0)),
                   pl.BlockSpec((1, d_ssm // 128, tm, 128), lambda b, i: (b, 0, i, 0))],
        out_shape=[jax.ShapeDtypeStruct((bsz, N_HEADS, seq // ATT_BLK, HEAD_PAD, ATT_BLK), BF16),
                   jax.ShapeDtypeStruct((bsz, N_HEADS, seq, HEAD_PAD), BF16),
                   jax.ShapeDtypeStruct((bsz, N_HEADS, seq // ATT_BLK, V_PAD, ATT_BLK), BF16),
                   jax.ShapeDtypeStruct((bsz, d_ssm // 128, seq, 128), F32)],
        compiler_params=_params("arbitrary", "arbitrary"),
        name="inproj",
    )(x, mod, ln1_g.reshape(1, d), w_in_r, q_norm_g.reshape(1, -1), w_uqt,
      kv_norm_g.reshape(1, -1), w_uk, w_uvt, positions.astype(F32).reshape(bsz, 1, seq), inv_freq.reshape(half, 1))


def _attn_kernel(qt_ref, k_ref, vt_ref, o_ref, m_sc, acc_sc, s_sc, *, blk, nq, dv):
    qi = pl.program_id(2)
    m_sc[...] = jnp.full(m_sc.shape, -jnp.inf, F32)
    acc_sc[...] = jnp.zeros(acc_sc.shape, F32)

    def scores(j, chains):
        k = k_ref[0, 0, pl.ds(pl.multiple_of(j * blk, blk), blk), :]
        return [_dot(k, qt_ref[0, 0, c]) for c in chains]

    def softmax_values(j, chains, s_list, diag):
        vt = vt_ref[0, 0, j]
        probs = []
        for c, s in zip(chains, s_list):
            if c == diag:
                key = lax.broadcasted_iota(jnp.int32, s.shape, 0)
                qry = lax.broadcasted_iota(jnp.int32, s.shape, 1)
                s = jnp.where(key <= qry, s, -1e30)
            m = m_sc[c]
            m_new = jnp.maximum(m, jnp.max(s, axis=0, keepdims=True))
            alpha = jnp.exp2(m - m_new)
            p = jnp.exp2(s - m_new)
            m_sc[c] = m_new
            probs.append((alpha, p.astype(BF16)))
        for c, (alpha, p) in zip(chains, probs):
            acc_sc[c] = alpha * acc_sc[c] + _dot(vt, p)

    every = range(nq)
    nfull = qi * nq

    def step(j, slot):
        for c, s in zip(every, scores(j + 1, every)):
            s_sc[1 - slot, c] = s
        softmax_values(j, every, [s_sc[slot, c] for c in every], None)

    def body(i, carry):
        for t in range(ATT_UNROLL):
            step(ATT_UNROLL * i + t, t % 2)
        return carry

    for c, s in zip(every, scores(0, every)):
        s_sc[0, c] = s
    lax.fori_loop(0, nfull // ATT_UNROLL, body, 0)
    s_cur = [s_sc[0, c] for c in every]
    for kc in range(nq):
        s_next = scores(nfull + kc + 1, range(kc + 1, nq)) if kc + 1 < nq else None
        softmax_values(nfull + kc, range(kc, nq), s_cur, kc)
        s_cur = s_next
    for c in range(nq):
        acc = acc_sc[c]
        o_ref[0, blk * c:blk * (c + 1), :] = (acc[:dv] / acc[dv:dv + 1]).T.astype(o_ref.dtype)


def _attention(qt, k, vt):
    bsz, nh, nblk, dh, blk = qt.shape
    dvp = vt.shape[3]
    dv = V_HEAD
    seq = nblk * blk
    nq = min(ATT_CHAINS, nblk)
    assert nq % ATT_UNROLL == 0 and ATT_UNROLL % 2 == 0 and nblk % nq == 0
    return pl.pallas_call(
        functools.partial(_attn_kernel, blk=blk, nq=nq, dv=dv),
        grid=(bsz, nh, nblk // nq),
        in_specs=[pl.BlockSpec((1, 1, nq, dh, blk), lambda b, h, i: (b, h, i, 0, 0)),
                  pl.BlockSpec((1, 1, seq, dh), lambda b, h, i: (b, h, 0, 0)),
                  pl.BlockSpec((1, 1, nblk, dvp, blk), lambda b, h, i: (b, h, 0, 0, 0))],
        out_specs=pl.BlockSpec((1, nq * blk, dv), lambda b, h, i: (b, i, h)),
        out_shape=jax.ShapeDtypeStruct((bsz, seq, nh * dv), BF16),
        scratch_shapes=[pltpu.VMEM((nq, 1, blk), F32), pltpu.VMEM((nq, dvp, blk), F32),
                        pltpu.VMEM((2, nq, blk, blk), F32)],
        compiler_params=_params("arbitrary", "arbitrary", "arbitrary"),
        name="attention",
    )(qt, k, vt)


def _slab_rows(slab, g8, s):
    g = 8 * slab + g8
    return g // 2, (g % 2) * SSM_CHUNK * SSM_GROUP + SSM_GROUP * s


def _regroup_kernel(u_ref, z_ref):
    nslab = u_ref.shape[1]
    nc = u_ref.shape[2] // SSM_CHUNK
    for slab in range(nslab):
        for s in range(SSM_CHUNK):
            t = u_ref[0, slab, pl.ds(s, nc, stride=SSM_CHUNK), :].T.astype(BF16)
            for g8 in range(128 // SSM_GROUP):
                pair, row = _slab_rows(slab, g8, s)
                z_ref[pair, row:row + SSM_GROUP, :] = t[SSM_GROUP * g8:SSM_GROUP * (g8 + 1), :]


def _regroup(u):
    bsz, nslab, seq, _ = u.shape
    nck = seq // SSM_CHUNK
    npair = nslab * 128 // SSM_GROUP // 2
    width = 2 * SSM_CHUNK * SSM_GROUP
    nt = seq // SSM_TOK
    return pl.pallas_call(
        _regroup_kernel,
        grid=(bsz, nt),
        in_specs=[pl.BlockSpec((1, nslab, SSM_TOK, 128), lambda b, i: (b, 0, i, 0))],
        out_specs=pl.BlockSpec((npair, width, SSM_TOK // SSM_CHUNK), lambda b, i: (0, 0, b * nt + i)),
        out_shape=jax.ShapeDtypeStruct((npair, width, bsz * nck), BF16),
        compiler_params=_params("arbitrary", "arbitrary"),
        name="regroup",
    )(u)


def _ungroup_kernel(yt_ref, y_ref):
    nslab = y_ref.shape[1]
    nc = y_ref.shape[2] // SSM_CHUNK
    for slab in range(nslab):
        for s in range(SSM_CHUNK):
            pieces = []
            for g8 in range(128 // SSM_GROUP):
                pair, row = _slab_rows(slab, g8, s)
                pieces.append(yt_ref[pair, row:row + SSM_GROUP, :])
            y_ref[0, slab, pl.ds(s, nc, stride=SSM_CHUNK), :] = jnp.concatenate(pieces, axis=0).astype(F32).T


def _ungroup(yt, bsz):
    npair, width, ncol = yt.shape
    seq = ncol // bsz * SSM_CHUNK
    nslab = npair * 2 * SSM_GROUP // 128
    nt = seq // SSM_TOK
    return pl.pallas_call(
        _ungroup_kernel,
        grid=(bsz, nt),
        in_specs=[pl.BlockSpec((npair, width, SSM_TOK // SSM_CHUNK), lambda b, i: (0, 0, b * nt + i))],
        out_specs=pl.BlockSpec((1, nslab, SSM_TOK, 128), lambda b, i: (b, 0, i, 0)),
        out_shape=jax.ShapeDtypeStruct((bsz, nslab, seq, 128), F32),
        compiler_params=_params("arbitrary", "arbitrary"),
        name="ungroup",
    )(yt)


def _ssm_kernel(z_ref, toep_ref, r_ref, o_ref, are_ref, aim_ref, y_ref, rt_sc, xp_sc, *, bsz, nck):
    ncol = bsz * nck
    cb = min(512, ncol)
    ns = are_ref.shape[-1]
    gw = z_ref.shape[1] // 2
    zg = lambda g2, c0: z_ref[0, gw * g2:gw * (g2 + 1), c0:c0 + cb]
    for c0 in range(0, ncol, cb):
        r0 = _dot(r_ref[0], zg(0, c0))
        r1 = _dot(r_ref[1], zg(1, c0))
        r_re = jnp.concatenate([r0[:ns // 2], r1[:ns // 2]], axis=0)
        r_im = jnp.concatenate([r0[ns // 2:], r1[ns // 2:]], axis=0)
        for q in range(0, cb, 128):
            rt_sc[0, c0 + q:c0 + q + 128, :] = r_re[:, q:q + 128].T
            rt_sc[1, c0 + q:c0 + q + 128, :] = r_im[:, q:q + 128].T
    a_re = jnp.broadcast_to(are_ref[0], (bsz, ns))
    a_im = jnp.broadcast_to(aim_ref[0], (bsz, ns))

    def step(c, carry):
        s_re, s_im = carry
        rows = pl.ds(c, bsz, stride=nck)
        xp_sc[0, rows, :] = s_re
        xp_sc[1, rows, :] = s_im
        n_re = a_re * s_re - a_im * s_im + rt_sc[0, rows, :]
        n_im = a_re * s_im + a_im * s_re + rt_sc[1, rows, :]
        return n_re, n_im

    zero = jnp.zeros((bsz, ns), F32)
    lax.fori_loop(0, nck, step, (zero, zero), unroll=4)
    for c0 in range(0, ncol, cb):
        xp = jnp.concatenate([xp_sc[0, c0:c0 + cb, :], xp_sc[1, c0:c0 + cb, :]], axis=-1).astype(BF16)
        y = lax.dot_general(o_ref[0], xp, _NT, preferred_element_type=F32)
        y += jnp.concatenate([_dot(toep_ref[0], zg(0, c0)), _dot(toep_ref[1], zg(1, c0))], axis=0)
        y_ref[0, :, c0:c0 + cb] = y.astype(y_ref.dtype)


def _ssm(z, toep, r, o, a_re, a_im, bsz):
    npair, width, ncol = z.shape
    ns = a_re.shape[-1]
    blk = lambda a: pl.BlockSpec((1,) + a.shape[1:], lambda g: (g,) + (0,) * (a.ndim - 1))
    two = lambda a: pl.BlockSpec((2,) + a.shape[1:], lambda g: (g,) + (0,) * (a.ndim - 1))
    return pl.pallas_call(
        functools.partial(_ssm_kernel, bsz=bsz, nck=ncol // bsz),
        grid=(npair,),
        in_specs=[blk(z), two(toep), two(r), blk(o), blk(a_re), blk(a_im)],
        out_specs=pl.BlockSpec((1, width, ncol), lambda g: (g, 0, 0)),
        out_shape=jax.ShapeDtypeStruct((npair, width, ncol), BF16),
        scratch_shapes=[pltpu.VMEM((2, ncol, ns), F32)] * 2,
        compiler_params=_params("arbitrary"),
        name="ssm",
    )(z, toep, r, o, a_re, a_im)


def _ssm_operators(lam_re, lam_im, log_dt, b_re, b_im, c_re, c_im, d_skip):
    t_len = SSM_CHUNK
    g, p = lam_re.shape
    hch = b_re.shape[-1]
    lr = jnp.minimum(lam_re.astype(F32), -1e-4)
    li = lam_im.astype(F32)
    dt = jnp.exp(log_dt.astype(F32))[:, None]
    mag = jnp.exp(lr * dt)
    ab_re = mag * jnp.cos(li * dt)
    ab_im = mag * jnp.sin(li * dt)
    den = lr * lr + li * li
    nr, ni = ab_re - 1.0, ab_im
    z_re = ((nr * lr + ni * li) / den)[..., None]
    z_im = ((ni * lr - nr * li) / den)[..., None]
    br, bi = b_re.astype(F32), b_im.astype(F32)
    bb_re = z_re * br - z_im * bi
    bb_im = z_re * bi + z_im * br
    tau = jnp.arange(t_len + 1, dtype=F32)[None, :, None]
    pm = jnp.exp(tau * (lr * dt)[:, None, :])
    pw_re = pm * jnp.cos(tau * (li * dt)[:, None, :])
    pw_im = pm * jnp.sin(tau * (li * dt)[:, None, :])
    cr, ci = c_re.astype(F32)[:, None], c_im.astype(F32)[:, None]
    ca_re = cr * pw_re[:, :, None, :] - ci * pw_im[:, :, None, :]
    ca_im = cr * pw_im[:, :, None, :] + ci * pw_re[:, :, None, :]
    ca = jnp.concatenate([ca_re[:, :t_len], -ca_im[:, :t_len]], axis=-1).transpose(0, 2, 1, 3)
    bb = jnp.concatenate([bb_re, bb_im], axis=1).transpose(0, 2, 1)
    kern = jnp.sum(ca[:, :, :, None, :] * bb[:, None, None, :, :], axis=-1)
    kern = kern.at[:, :, 0, :].add(jnp.eye(hch, dtype=F32)[None] * d_skip.astype(F32)[:, :, None])
    krow = jnp.concatenate([kern[:, :, ::-1, :].reshape(g, hch, t_len * hch),
                            jnp.zeros((g, hch, (t_len - 1) * hch), F32)], axis=-1)
    toep = jnp.stack([krow[:, :, (t_len - 1 - t) * hch:(2 * t_len - 1 - t) * hch] for t in range(t_len)], axis=1)
    toep = toep.reshape(g, t_len * hch, t_len * hch)
    rp_re = pw_re[:, :t_len][:, ::-1].transpose(0, 2, 1)[..., None]
    rp_im = pw_im[:, :t_len][:, ::-1].transpose(0, 2, 1)[..., None]
    rr = (rp_re * bb_re[:, :, None, :] - rp_im * bb_im[:, :, None, :]).reshape(g, p, t_len * hch)
    ri = (rp_re * bb_im[:, :, None, :] + rp_im * bb_re[:, :, None, :]).reshape(g, p, t_len * hch)
    orr = ca_re[:, 1:].reshape(g, t_len * hch, p)
    oii = (-ca_im[:, 1:]).reshape(g, t_len * hch, p)

    def pair_diag(m):
        m = m.reshape(g // 2, 2, m.shape[1], m.shape[2])
        z = jnp.zeros_like(m[:, 0])
        top = jnp.concatenate([m[:, 0], z], axis=2)
        bot = jnp.concatenate([z, m[:, 1]], axis=2)
        return jnp.concatenate([top, bot], axis=1).astype(BF16)

    a_t_re = pw_re[:, t_len].reshape(g // 2, 1, 2 * p)
    a_t_im = pw_im[:, t_len].reshape(g // 2, 1, 2 * p)
    r_g = jnp.concatenate([rr, ri], axis=1).astype(BF16)
    o_pair = jnp.concatenate([pair_diag(orr), pair_diag(oii)], axis=2)
    return toep.astype(BF16), r_g, o_pair, a_t_re, a_t_im


def _mixer_output(y_ref, a_ref, x_ref, mod_ref, wglu_ref, bglu_ref, ag_ref, sg_ref, wout_ref):
    d_ssm = y_ref.shape[1] * y_ref.shape[3]
    d_attn = a_ref.shape[-1]
    tm = x_ref.shape[1]
    parts = [slice(r, r + tm // MIX_SPLIT) for r in range(0, tm, tm // MIX_SPLIT)]
    gls = []
    for rows in parts:
        y = jnp.concatenate([y_ref[0, slab, rows, :] for slab in range(y_ref.shape[1])], axis=-1)
        gls.append(_dot(_gelu_sigmoid(y).astype(BF16), wglu_ref[...]) + bglu_ref[...])
    outs = []
    for rows, gl in zip(parts, gls):
        s = gl[:, :d_ssm] * _sigmoid(gl[:, d_ssm:])
        sn = _rms(s, sg_ref[...]).astype(BF16)
        an = _rms(a_ref[0, rows, :].astype(F32), ag_ref[...]).astype(BF16)
        m = _dot(an, wout_ref[:d_attn, :]) + _dot(sn, wout_ref[d_attn:, :])
        outs.append(x_ref[0, rows, :] + mod_ref[0, 2:3, :] * m)
    return jnp.concatenate(outs, axis=0)


def _channel_mixer(x, gate_halo, mod_ref, ln_ref, wup_ref, cw_ref, cb_ref, wdn_ref, fg_ref, *, n_chunks, last):
    tm = x.shape[0]
    d_ff = wdn_ref.shape[0]
    h = (_rms(x, ln_ref[...]) * (1.0 + mod_ref[0, 4:5, :]) + mod_ref[0, 3:4, :]).astype(BF16)

    def up_proj(j):
        cols = slice(FF_CHUNK * j, FF_CHUNK * (j + 1))
        vals = slice(d_ff + FF_CHUNK * j, d_ff + FF_CHUNK * (j + 1))
        return _dot(h, wup_ref[:, cols]), _dot(h, wup_ref[:, vals])

    up = up_proj(0)
    acc = None
    acts = []
    for j in range(n_chunks):
        up_next = up_proj(j + 1) if j + 1 < n_chunks else None
        cols = slice(FF_CHUNK * j, FF_CHUNK * (j + 1))
        gate = jnp.concatenate([gate_halo[:, cols], up[0]], axis=0)
        gate_halo[:, cols] = up[0][tm - HALO:]
        conv = cw_ref[0:1, cols] * gate
        for t in range(1, CONV_W):
            conv = cw_ref[t:t + 1, cols] * gate + pltpu.roll(conv, 1, 0)
        conv = conv[HALO:] + cb_ref[:, cols]
        acts.append((_gelu_sigmoid(conv) * up[1]).astype(BF16))
        if len(acts) == FF_GROUP or j + 1 == n_chunks:
            rows = slice(FF_CHUNK * (j + 1 - len(acts)), FF_CHUNK * (j + 1))
            down = _dot(jnp.concatenate(acts, axis=-1), wdn_ref[rows, :])
            acc = down if acc is None else acc + down
            acts = []
        up = up_next
    x2 = x + mod_ref[0, 5:6, :] * acc
    return _rms(x2, fg_ref[...]) if last else x2


def _mixffn_kernel(y_ref, a_ref, x_ref, mod_ref, wglu_ref, bglu_ref, ag_ref, sg_ref, wout_ref,
                   ln_ref, wup_ref, cw_ref, cb_ref, wdn_ref, fg_ref, o_ref, gate_halo, *, n_chunks, last):
    @pl.when(pl.program_id(1) == 0)
    def _():
        gate_halo[...] = jnp.zeros(gate_halo.shape, F32)

    x1 = _mixer_output(y_ref, a_ref, x_ref, mod_ref, wglu_ref, bglu_ref, ag_ref, sg_ref, wout_ref)
    o_ref[0] = _channel_mixer(x1, gate_halo, mod_ref, ln_ref, wup_ref, cw_ref, cb_ref, wdn_ref, fg_ref,
                              n_chunks=n_chunks, last=last)


def _mixffn(y, a, x, mod, w_glu, b_glu, attn_g, ssm_g, w_out, ln2_g, w_up_r, conv_w, conv_b, w_down, final_g,
            tm, last):
    bsz, seq, d = x.shape
    nslab = y.shape[1]
    d_ssm, d_attn = nslab * y.shape[3], a.shape[-1]
    d_ff = w_down.shape[0]
    n_chunks = d_ff // FF_CHUNK
    const = lambda shape: pl.BlockSpec(shape, lambda b, i: (0,) * len(shape))
    weight = lambda w: pl.BlockSpec(w.shape, lambda b, i: (0,) * w.ndim, pipeline_mode=pl.Buffered(1))
    tile = lambda w: pl.BlockSpec((1, tm, w), lambda b, i: (b, i, 0))
    return pl.pallas_call(
        functools.partial(_mixffn_kernel, n_chunks=n_chunks, last=last),
        grid=(bsz, seq // tm),
        in_specs=[pl.BlockSpec((1, nslab, tm, 128), lambda b, i: (b, 0, i, 0)), tile(d_attn), tile(d),
                  pl.BlockSpec((1, 6, d), lambda b, i: (b, 0, 0)),
                  weight(w_glu), const((1, 2 * d_ssm)), const((1, d_attn)), const((1, d_ssm)), weight(w_out),
                  const((1, d)), weight(w_up_r), const(conv_w.shape), const(conv_b.shape), weight(w_down),
                  const((1, d))],
        out_specs=tile(d),
        out_shape=jax.ShapeDtypeStruct((bsz, seq, d), F32),
        scratch_shapes=[pltpu.VMEM((HALO, d_ff), F32)],
        compiler_params=_params("arbitrary", "arbitrary"),
        name="mixffn",
    )(y, a, x, mod, w_glu, b_glu.reshape(1, -1), attn_g.reshape(1, -1), ssm_g.reshape(1, -1), w_out,
      ln2_g.reshape(1, d), w_up_r, conv_w, conv_b, w_down, final_g.reshape(1, d))


def _layer(x, mod, positions, w_in, ln1_g, q_norm_g, w_uq, kv_norm_g, w_ukv, ssm, w_glu, b_glu,
           attn_out_g, ssm_out_g, w_out, ln2_g, w_up, conv_w, conv_b, w_down, final_g, last):
    bsz, seq, d = x.shape
    tm = min(512, seq)
    half = QK_ROPE // 2
    assert seq % SSM_TOK == 0 and seq % tm == 0 and tm % ATT_BLK == 0 and tm % (HALO * MIX_SPLIT) == 0
    assert d % 128 == 0 and w_down.shape[0] % FF_CHUNK == 0
    o_kr = Q_LORA + KV_LORA
    o_u = o_kr + QK_ROPE
    d_ssm = w_in.shape[1] - o_u
    kr_w = w_in[:, o_kr:o_u]
    w_in_r = jnp.concatenate([w_in[:, :o_kr], w_in[:, o_u:], kr_w, kr_w], axis=1).astype(BF16)
    w_uqt = w_uq.T.astype(BF16)
    ukv = w_ukv.reshape(KV_LORA, N_HEADS, QK_NOPE + V_HEAD)
    w_uk = ukv[:, :, :QK_NOPE].reshape(KV_LORA, -1).astype(BF16)
    w_uvt = ukv[:, :, QK_NOPE:].reshape(KV_LORA, -1).T.astype(BF16)

    qt, k, vt, u = _inproj(x, mod, ln1_g, w_in_r, q_norm_g, w_uqt, kv_norm_g, w_uk, w_uvt, positions, tm)
    a = _attention(qt, k, vt)

    y = _ungroup(_ssm(_regroup(u), *ssm, bsz), bsz)

    return _mixffn(y, a, x, mod, w_glu.astype(BF16), b_glu, attn_out_g, ssm_out_g, w_out.astype(BF16), ln2_g,
                   w_up.astype(BF16), conv_w, conv_b.reshape(1, -1), w_down.astype(BF16), final_g, tm, last)


def kernel(x, c, positions, w_mod, b_mod, ln1_g, w_in, q_norm_g, w_uq, kv_norm_g, w_ukv, ssm_lam_re, ssm_lam_im, ssm_log_dt, ssm_b_re, ssm_b_im, ssm_c_re, ssm_c_im, ssm_d, w_glu, b_glu, attn_out_g, ssm_out_g, w_out, ln2_g, w_up, conv_w, conv_b, w_down, final_g):
    bsz, seq, d = x.shape
    depth = w_in.shape[0]
    for l in range(depth):
        mod = _mod(c, w_mod[l], b_mod[l]).reshape(bsz, 6, d)
        ssm = _ssm_operators(ssm_lam_re[l], ssm_lam_im[l], ssm_log_dt[l], ssm_b_re[l], ssm_b_im[l],
                             ssm_c_re[l], ssm_c_im[l], ssm_d[l])
        x = _layer(x, mod, positions, w_in[l], ln1_g[l], q_norm_g[l], w_uq[l], kv_norm_g[l], w_ukv[l], ssm,
                   w_glu[l], b_glu[l], attn_out_g[l], ssm_out_g[l], w_out[l], ln2_g[l], w_up[l], conv_w[l],
                   conv_b[l], w_down[l], final_g, l == depth - 1)
    return x
```

```python
import functools
import math

import jax
import jax.numpy as jnp
from jax import lax
from jax.experimental import pallas as pl
from jax.experimental.pallas import tpu as pltpu

N_HEADS = 4
QK_NOPE = 128
QK_ROPE = 64
V_HEAD = 128
Q_LORA = 384
KV_LORA = 256
ROPE_THETA = 10000.0
SSM_GROUP = 16
SSM_STATE = 64
CONV_W = 3
EPS = 1e-6

HEAD_PAD = 256
V_PAD = V_HEAD + 16
ATT_BLK = 256
ATT_CHAINS = 8
ATT_UNROLL = 8
SSM_CHUNK = 16
SSM_TOK = 2048
FF_CHUNK = 256
FF_GROUP = 6
MIX_SPLIT = 2
MIX_TILE = 1024
HALO = 8
VMEM_LIMIT = 56 * 1024 * 1024

F32 = jnp.float32
BF16 = jnp.bfloat16


def _rms(x, g):
    return x * lax.rsqrt(jnp.mean(x * x, axis=-1, keepdims=True) + EPS) * g


def _gelu(x):
    return 0.5 * x * (1.0 + jnp.tanh(math.sqrt(2.0 / math.pi) * (x + 0.044715 * (x * x * x))))


def _gelu_sigmoid(x):
    a = -2.0 * math.sqrt(2.0 / math.pi) * math.log2(math.e)
    return x * (1.0 / (1.0 + jnp.exp2(x * (a + (a * 0.044715) * (x * x)))))


def _sigmoid(x):
    return 1.0 / (1.0 + jnp.exp(-x))


def _dot(a, b):
    return jnp.dot(a, b, preferred_element_type=F32)


def _params(*sem, flags=None):
    return pltpu.CompilerParams(dimension_semantics=sem, vmem_limit_bytes=VMEM_LIMIT, flags=flags)


def _mod_kernel(c_ref, w_ref, b_ref, o_ref):
    c = c_ref[...]
    cond = c * _sigmoid(c)
    o_ref[...] = jnp.dot(cond, w_ref[...], preferred_element_type=F32,
                         precision=lax.Precision.HIGHEST) + b_ref[...]


def _mod(c, w_mod, b_mod):
    bsz, d = c.shape
    n = w_mod.shape[1]
    tn = 1024
    return pl.pallas_call(
        _mod_kernel,
        grid=(n // tn,),
        in_specs=[pl.BlockSpec((bsz, d), lambda j: (0, 0)),
                  pl.BlockSpec((d, tn), lambda j: (0, j)),
                  pl.BlockSpec((1, tn), lambda j: (0, j))],
        out_specs=pl.BlockSpec((bsz, tn), lambda j: (0, j)),
        out_shape=jax.ShapeDtypeStruct((bsz, n), F32),
        compiler_params=_params("arbitrary"),
        name="mod",
    )(c, w_mod, b_mod.reshape(1, n))


_NT = (((1,), (1,)), ((), ()))


def _inproj_kernel(x_ref, mod_ref, ln_ref, win_ref, qg_ref, wuqt_ref, kvg_ref, wuk_ref, wuvt_ref,
                   pos_ref, freq_ref, qt_ref, k_ref, vt_ref, u_ref, *, scale, tm):
    x = x_ref[0]
    h = _rms(x, ln_ref[...]) * (1.0 + mod_ref[0, 1:2, :]) + mod_ref[0, 0:1, :]
    z = _dot(h.astype(BF16), win_ref[...])
    o_kv = Q_LORA
    o_u = o_kv + KV_LORA
    o_kr = z.shape[1] - 128
    for slab in range(u_ref.shape[1]):
        u_ref[0, slab] = z[:, o_u + 128 * slab:o_u + 128 * (slab + 1)]
    zqn = _rms(z[:, :o_kv], qg_ref[...]).astype(BF16)
    zkvn = _rms(z[:, o_kv:o_u], kvg_ref[...]).astype(BF16)
    qt = lax.dot_general(wuqt_ref[...], zqn, _NT, preferred_element_type=F32)
    vt = lax.dot_general(wuvt_ref[...], zkvn, _NT, preferred_element_type=F32)
    kn = _dot(zkvn, wuk_ref[...])
    ang = freq_ref[...] * pos_ref[0]
    cos_t = jnp.cos(ang)
    sin_t = jnp.sin(ang)
    half = QK_ROPE // 2
    zero_t = jnp.zeros((128 - QK_ROPE, tm), F32)
    cc = jnp.concatenate([cos_t, cos_t, zero_t], axis=0).T
    ss = jnp.concatenate([-sin_t, sin_t, zero_t], axis=0).T
    zkr = z[:, o_kr:]
    kr = (zkr * cc + pltpu.roll(zkr, half, 1) * ss).astype(BF16)
    ones_rows = (lax.broadcasted_iota(jnp.int32, (V_PAD - V_HEAD, tm), 0) == 0).astype(F32)
    for hd in range(N_HEADS):
        base = (QK_NOPE + QK_ROPE) * hd
        r1 = qt[base + QK_NOPE:base + QK_NOPE + half]
        r2 = qt[base + QK_NOPE + half:base + QK_NOPE + QK_ROPE]
        head = jnp.concatenate([qt[base:base + QK_NOPE], r1 * cos_t - r2 * sin_t, r2 * cos_t + r1 * sin_t,
                                zero_t[:HEAD_PAD - QK_NOPE - QK_ROPE]], axis=0)
        head = (head * scale).astype(BF16)
        vth = jnp.concatenate([vt[V_HEAD * hd:V_HEAD * (hd + 1)], ones_rows], axis=0).astype(BF16)
        for jj in range(tm // ATT_BLK):
            qt_ref[0, hd, jj] = head[:, ATT_BLK * jj:ATT_BLK * (jj + 1)]
            vt_ref[0, hd, jj] = vth[:, ATT_BLK * jj:ATT_BLK * (jj + 1)]
        k_ref[0, hd, :, :QK_NOPE] = kn[:, QK_NOPE * hd:QK_NOPE * (hd + 1)].astype(BF16)
        k_ref[0, hd, :, QK_NOPE:] = kr


def _inproj(x, mod, ln1_g, w_in_r, q_norm_g, w_uqt, kv_norm_g, w_uk, w_uvt, positions, tm):
    bsz, seq, d = x.shape
    d_ssm = w_in_r.shape[1] - Q_LORA - KV_LORA - 128
    half = QK_ROPE // 2
    inv_freq = ROPE_THETA ** (-jnp.arange(0, QK_ROPE, 2, dtype=F32) / QK_ROPE)
    nb = tm // ATT_BLK
    const = lambda shape: pl.BlockSpec(shape, lambda b, i: (0,) * len(shape))
    scale = (QK_NOPE + QK_ROPE) ** -0.5 * math.log2(math.e)
    return pl.pallas_call(
        functools.partial(_inproj_kernel, scale=scale, tm=tm),
        grid=(bsz, seq // tm),
        in_specs=[pl.BlockSpec((1, tm, d), lambda b, i: (b, i, 0)),
                  pl.BlockSpec((1, 6, d), lambda b, i: (b, 0, 0)),
                  const((1, d)), const(w_in_r.shape), const((1, Q_LORA)), const(w_uqt.shape),
                  const((1, KV_LORA)), const(w_uk.shape), const(w_uvt.shape),
                  pl.BlockSpec((1, 1, tm), lambda b, i: (b, 0, i)), const((half, 1))],
        out_specs=[pl.BlockSpec((1, N_HEADS, nb, HEAD_PAD, ATT_BLK), lambda b, i: (b, 0, i, 0, 0)),
                   pl.BlockSpec((1, N_HEADS, tm, HEAD_PAD), lambda b, i: (b, 0, i, 0)),
                   pl.BlockSpec((1, N_HEADS, nb, V_PAD, ATT_BLK), lambda b, i: (b, 0, i, 0, 0)),
                   pl.BlockSpec((1, d_ssm // 128, tm, 128), lambda b, i: (b, 0, i, 0))],
        out_shape=[jax.ShapeDtypeStruct((bsz, N_HEADS, seq // ATT_BLK, HEAD_PAD, ATT_BLK), BF16),
                   jax.ShapeDtypeStruct((bsz, N_HEADS, seq, HEAD_PAD), BF16),
                   jax.ShapeDtypeStruct((bsz, N_HEADS, seq // ATT_BLK, V_PAD, ATT_BLK), BF16),
                   jax.ShapeDtypeStruct((bsz, d_ssm // 128, seq, 128), F32)],
        compiler_params=_params("arbitrary", "arbitrary"),
        name="inproj",
    )(x, mod, ln1_g.reshape(1, d), w_in_r, q_norm_g.reshape(1, -1), w_uqt,
      kv_norm_g.reshape(1, -1), w_uk, w_uvt, positions.astype(F32).reshape(bsz, 1, seq), inv_freq.reshape(half, 1))


def _attn_kernel(qt_ref, k_ref, vt_ref, o_ref, m_sc, acc_sc, s_sc, *, blk, nq, dv):
    qi = pl.program_id(2)
    m_sc[...] = jnp.full(m_sc.shape, -jnp.inf, F32)
    acc_sc[...] = jnp.zeros(acc_sc.shape, F32)

    def key_rows(j):
        return k_ref[0, 0, pl.ds(pl.multiple_of(j * blk, blk), blk), :]

    def scores(j, chains):
        k = key_rows(j)
        return [_dot(k, qt_ref[0, 0, c]) for c in chains]

    def softmax_values(items):
        probs = []
        for c, j, s, masked in items:
            if masked:
                key = lax.broadcasted_iota(jnp.int32, s.shape, 0)
                qry = lax.broadcasted_iota(jnp.int32, s.shape, 1)
                s = jnp.where(key <= qry, s, -1e30)
            m = m_sc[c]
            m_new = jnp.maximum(m, jnp.max(s, axis=0, keepdims=True))
            alpha = jnp.exp2(m - m_new)
            p = jnp.exp2(s - m_new)
            m_sc[c] = m_new
            probs.append((alpha, p.astype(BF16)))
        for (c, j, _, _), (alpha, p) in zip(items, probs):
            acc_sc[c] = alpha * acc_sc[c] + _dot(vt_ref[0, 0, j], p)

    every = range(nq)
    nfull = qi * nq

    def step(j, slot):
        for c, s in zip(every, scores(j + 1, every)):
            s_sc[1 - slot, c] = s
        softmax_values([(c, j, s_sc[slot, c], False) for c in every])

    def body(i, carry):
        for t in range(ATT_UNROLL):
            step(ATT_UNROLL * i + t, t % 2)
        return carry

    for c, s in zip(every, scores(0, every)):
        s_sc[0, c] = s
    lax.fori_loop(0, nfull // ATT_UNROLL, body, 0)

    first = lambda c: 0 if c >= nq // 2 else nq - 1 - c
    rounds = [[(c, r - first(c)) for c in every if 0 <= r - first(c) <= c] for r in range(nq)]

    def issue(pairs):
        keys = {kc: key_rows(nfull + kc) for kc in sorted({kc for _, kc in pairs if kc > 0})}
        return [_dot(keys[kc], qt_ref[0, 0, c]) if kc > 0 else None for c, kc in pairs]

    s_cur = issue(rounds[0])
    for r in range(nq):
        s_next = issue(rounds[r + 1]) if r + 1 < nq else None
        softmax_values([(c, nfull + kc, s_sc[0, c] if s is None else s, kc == c)
                        for (c, kc), s in zip(rounds[r], s_cur)])
        s_cur = s_next
    for c in range(nq):
        acc = acc_sc[c]
        o_ref[0, blk * c:blk * (c + 1), :] = (acc[:dv] / acc[dv:dv + 1]).T.astype(o_ref.dtype)


def _attention(qt, k, vt):
    bsz, nh, nblk, dh, blk = qt.shape
    dvp = vt.shape[3]
    dv = V_HEAD
    seq = nblk * blk
    nq = min(ATT_CHAINS, nblk)
    assert nq % ATT_UNROLL == 0 and ATT_UNROLL % 2 == 0 and nblk % nq == 0
    return pl.pallas_call(
        functools.partial(_attn_kernel, blk=blk, nq=nq, dv=dv),
        grid=(bsz, nh, nblk // nq),
        in_specs=[pl.BlockSpec((1, 1, nq, dh, blk), lambda b, h, i: (b, h, i, 0, 0)),
                  pl.BlockSpec((1, 1, seq, dh), lambda b, h, i: (b, h, 0, 0)),
                  pl.BlockSpec((1, 1, nblk, dvp, blk), lambda b, h, i: (b, h, 0, 0, 0))],
        out_specs=pl.BlockSpec((1, nq * blk, dv), lambda b, h, i: (b, i, h)),
        out_shape=jax.ShapeDtypeStruct((bsz, seq, nh * dv), BF16),
        scratch_shapes=[pltpu.VMEM((nq, 1, blk), F32), pltpu.VMEM((nq, dvp, blk), F32),
                        pltpu.VMEM((2, nq, blk, blk), F32)],
        compiler_params=_params("arbitrary", "arbitrary", "arbitrary"),
        name="attention",
    )(qt, k, vt)


def _slab_rows(slab, g8, s):
    g = 8 * slab + g8
    return g // 2, (g % 2) * SSM_CHUNK * SSM_GROUP + SSM_GROUP * s


def _regroup_kernel(u_ref, z_ref):
    nslab = u_ref.shape[1]
    nc = u_ref.shape[2] // SSM_CHUNK
    for slab in range(nslab):
        for s in range(SSM_CHUNK):
            t = u_ref[0, slab, pl.ds(s, nc, stride=SSM_CHUNK), :].T.astype(BF16)
            for g8 in range(128 // SSM_GROUP):
                pair, row = _slab_rows(slab, g8, s)
                z_ref[pair, row:row + SSM_GROUP, :] = t[SSM_GROUP * g8:SSM_GROUP * (g8 + 1), :]


def _regroup(u):
    bsz, nslab, seq, _ = u.shape
    nck = seq // SSM_CHUNK
    npair = nslab * 128 // SSM_GROUP // 2
    width = 2 * SSM_CHUNK * SSM_GROUP
    nt = seq // SSM_TOK
    return pl.pallas_call(
        _regroup_kernel,
        grid=(bsz, nt),
        in_specs=[pl.BlockSpec((1, nslab, SSM_TOK, 128), lambda b, i: (b, 0, i, 0))],
        out_specs=pl.BlockSpec((npair, width, SSM_TOK // SSM_CHUNK), lambda b, i: (0, 0, b * nt + i)),
        out_shape=jax.ShapeDtypeStruct((npair, width, bsz * nck), BF16),
        compiler_params=_params("arbitrary", "arbitrary"),
        name="regroup",
    )(u)


def _ungroup_kernel(yt_ref, y_ref):
    nslab = y_ref.shape[1]
    nc = y_ref.shape[2] // SSM_CHUNK
    for slab in range(nslab):
        for s in range(SSM_CHUNK):
            pieces = []
            for g8 in range(128 // SSM_GROUP):
                pair, row = _slab_rows(slab, g8, s)
                pieces.append(yt_ref[pair, row:row + SSM_GROUP, :])
            y_ref[0, slab, pl.ds(s, nc, stride=SSM_CHUNK), :] = jnp.concatenate(pieces, axis=0).astype(F32).T


def _ungroup(yt, bsz):
    npair, width, ncol = yt.shape
    seq = ncol // bsz * SSM_CHUNK
    nslab = npair * 2 * SSM_GROUP // 128
    nt = seq // SSM_TOK
    return pl.pallas_call(
        _ungroup_kernel,
        grid=(bsz, nt),
        in_specs=[pl.BlockSpec((npair, width, SSM_TOK // SSM_CHUNK), lambda b, i: (0, 0, b * nt + i))],
        out_specs=pl.BlockSpec((1, nslab, SSM_TOK, 128), lambda b, i: (b, 0, i, 0)),
        out_shape=jax.ShapeDtypeStruct((bsz, nslab, seq, 128), F32),
        compiler_params=_params("arbitrary", "arbitrary"),
        name="ungroup",
    )(yt)


def _ssm_kernel(z_ref, toep_ref, r_ref, o_ref, are_ref, aim_ref, y_ref, rt_sc, xp_sc, *, bsz, nck):
    ncol = bsz * nck
    cb = min(512, ncol)
    ns = are_ref.shape[-1]
    gw = z_ref.shape[1] // 2
    zg = lambda g2, c0: z_ref[0, gw * g2:gw * (g2 + 1), c0:c0 + cb]
    for c0 in range(0, ncol, cb):
        r0 = _dot(r_ref[0], zg(0, c0))
        r1 = _dot(r_ref[1], zg(1, c0))
        r_re = jnp.concatenate([r0[:ns // 2], r1[:ns // 2]], axis=0)
        r_im = jnp.concatenate([r0[ns // 2:], r1[ns // 2:]], axis=0)
        for q in range(0, cb, 128):
            rt_sc[0, c0 + q:c0 + q + 128, :] = r_re[:, q:q + 128].T
            rt_sc[1, c0 + q:c0 + q + 128, :] = r_im[:, q:q + 128].T
    a_re = jnp.broadcast_to(are_ref[0], (bsz, ns))
    a_im = jnp.broadcast_to(aim_ref[0], (bsz, ns))

    def step(c, carry):
        s_re, s_im = carry
        rows = pl.ds(c, bsz, stride=nck)
        xp_sc[0, rows, :] = s_re
        xp_sc[1, rows, :] = s_im
        n_re = a_re * s_re - a_im * s_im + rt_sc[0, rows, :]
        n_im = a_re * s_im + a_im * s_re + rt_sc[1, rows, :]
        return n_re, n_im

    zero = jnp.zeros((bsz, ns), F32)
    lax.fori_loop(0, nck, step, (zero, zero), unroll=4)
    for c0 in range(0, ncol, cb):
        xp = jnp.concatenate([xp_sc[0, c0:c0 + cb, :], xp_sc[1, c0:c0 + cb, :]], axis=-1).astype(BF16)
        y = lax.dot_general(o_ref[0], xp, _NT, preferred_element_type=F32)
        y += jnp.concatenate([_dot(toep_ref[0], zg(0, c0)), _dot(toep_ref[1], zg(1, c0))], axis=0)
        y_ref[0, :, c0:c0 + cb] = y.astype(y_ref.dtype)


def _ssm(z, toep, r, o, a_re, a_im, bsz):
    npair, width, ncol = z.shape
    ns = a_re.shape[-1]
    blk = lambda a: pl.BlockSpec((1,) + a.shape[1:], lambda g: (g,) + (0,) * (a.ndim - 1))
    two = lambda a: pl.BlockSpec((2,) + a.shape[1:], lambda g: (g,) + (0,) * (a.ndim - 1))
    return pl.pallas_call(
        functools.partial(_ssm_kernel, bsz=bsz, nck=ncol // bsz),
        grid=(npair,),
        in_specs=[blk(z), two(toep), two(r), blk(o), blk(a_re), blk(a_im)],
        out_specs=pl.BlockSpec((1, width, ncol), lambda g: (g, 0, 0)),
        out_shape=jax.ShapeDtypeStruct((npair, width, ncol), BF16),
        scratch_shapes=[pltpu.VMEM((2, ncol, ns), F32)] * 2,
        compiler_params=_params("arbitrary"),
        name="ssm",
    )(z, toep, r, o, a_re, a_im)


def _ssm_operators(lam_re, lam_im, log_dt, b_re, b_im, c_re, c_im, d_skip):
    t_len = SSM_CHUNK
    g, p = lam_re.shape
    hch = b_re.shape[-1]
    lr = jnp.minimum(lam_re.astype(F32), -1e-4)
    li = lam_im.astype(F32)
    dt = jnp.exp(log_dt.astype(F32))[:, None]
    mag = jnp.exp(lr * dt)
    ab_re = mag * jnp.cos(li * dt)
    ab_im = mag * jnp.sin(li * dt)
    den = lr * lr + li * li
    nr, ni = ab_re - 1.0, ab_im
    z_re = ((nr * lr + ni * li) / den)[..., None]
    z_im = ((ni * lr - nr * li) / den)[..., None]
    br, bi = b_re.astype(F32), b_im.astype(F32)
    bb_re = z_re * br - z_im * bi
    bb_im = z_re * bi + z_im * br
    tau = jnp.arange(t_len + 1, dtype=F32)[None, :, None]
    pm = jnp.exp(tau * (lr * dt)[:, None, :])
    pw_re = pm * jnp.cos(tau * (li * dt)[:, None, :])
    pw_im = pm * jnp.sin(tau * (li * dt)[:, None, :])
    cr, ci = c_re.astype(F32)[:, None], c_im.astype(F32)[:, None]
    ca_re = cr * pw_re[:, :, None, :] - ci * pw_im[:, :, None, :]
    ca_im = cr * pw_im[:, :, None, :] + ci * pw_re[:, :, None, :]
    ca = jnp.concatenate([ca_re[:, :t_len], -ca_im[:, :t_len]], axis=-1).transpose(0, 2, 1, 3)
    bb = jnp.concatenate([bb_re, bb_im], axis=1).transpose(0, 2, 1)
    kern = jnp.sum(ca[:, :, :, None, :] * bb[:, None, None, :, :], axis=-1)
    kern = kern.at[:, :, 0, :].add(jnp.eye(hch, dtype=F32)[None] * d_skip.astype(F32)[:, :, None])
    krow = jnp.concatenate([kern[:, :, ::-1, :].reshape(g, hch, t_len * hch),
                            jnp.zeros((g, hch, (t_len - 1) * hch), F32)], axis=-1)
    toep = jnp.stack([krow[:, :, (t_len - 1 - t) * hch:(2 * t_len - 1 - t) * hch] for t in range(t_len)], axis=1)
    toep = toep.reshape(g, t_len * hch, t_len * hch)
    rp_re = pw_re[:, :t_len][:, ::-1].transpose(0, 2, 1)[..., None]
    rp_im = pw_im[:, :t_len][:, ::-1].transpose(0, 2, 1)[..., None]
    rr = (rp_re * bb_re[:, :, None, :] - rp_im * bb_im[:, :, None, :]).reshape(g, p, t_len * hch)
    ri = (rp_re * bb_im[:, :, None, :] + rp_im * bb_re[:, :, None, :]).reshape(g, p, t_len * hch)
    orr = ca_re[:, 1:].reshape(g, t_len * hch, p)
    oii = (-ca_im[:, 1:]).reshape(g, t_len * hch, p)

    def pair_diag(m):
        m = m.reshape(g // 2, 2, m.shape[1], m.shape[2])
        z = jnp.zeros_like(m[:, 0])
        top = jnp.concatenate([m[:, 0], z], axis=2)
        bot = jnp.concatenate([z, m[:, 1]], axis=2)
        return jnp.concatenate([top, bot], axis=1).astype(BF16)

    a_t_re = pw_re[:, t_len].reshape(g // 2, 1, 2 * p)
    a_t_im = pw_im[:, t_len].reshape(g // 2, 1, 2 * p)
    r_g = jnp.concatenate([rr, ri], axis=1).astype(BF16)
    o_pair = jnp.concatenate([pair_diag(orr), pair_diag(oii)], axis=2)
    return toep.astype(BF16), r_g, o_pair, a_t_re, a_t_im


def _mixer_output(y_ref, a_ref, x_ref, mod_ref, wglu_ref, bglu_ref, ag_ref, sg_ref, wout_ref):
    d_ssm = y_ref.shape[1] * y_ref.shape[3]
    d_attn = a_ref.shape[-1]
    tm = x_ref.shape[1]
    parts = [slice(r, r + tm // MIX_SPLIT) for r in range(0, tm, tm // MIX_SPLIT)]
    gls = []
    for rows in parts:
        y = jnp.concatenate([y_ref[0, slab, rows, :] for slab in range(y_ref.shape[1])], axis=-1)
        gls.append(_dot(_gelu_sigmoid(y).astype(BF16), wglu_ref[...]) + bglu_ref[...])
    outs = []
    for rows, gl in zip(parts, gls):
        s = gl[:, :d_ssm] * _sigmoid(gl[:, d_ssm:])
        sn = _rms(s, sg_ref[...]).astype(BF16)
        an = _rms(a_ref[0, rows, :].astype(F32), ag_ref[...]).astype(BF16)
        m = _dot(an, wout_ref[:d_attn, :]) + _dot(sn, wout_ref[d_attn:, :])
        outs.append(x_ref[0, rows, :] + mod_ref[0, 2:3, :] * m)
    return jnp.concatenate(outs, axis=0)


def _channel_mixer(x, gate_halo, mod_ref, ln_ref, wup_ref, cw_ref, cb_ref, wdn_ref, fg_ref, *, n_chunks, last):
    tm = x.shape[0]
    d_ff = wdn_ref.shape[0]
    h = (_rms(x, ln_ref[...]) * (1.0 + mod_ref[0, 4:5, :]) + mod_ref[0, 3:4, :]).astype(BF16)

    def up_proj(j):
        cols = slice(FF_CHUNK * j, FF_CHUNK * (j + 1))
        vals = slice(d_ff + FF_CHUNK * j, d_ff + FF_CHUNK * (j + 1))
        return _dot(h, wup_ref[:, cols]), _dot(h, wup_ref[:, vals])

    up = up_proj(0)
    acc = None
    acts = []
    for j in range(n_chunks):
        up_next = up_proj(j + 1) if j + 1 < n_chunks else None
        cols = slice(FF_CHUNK * j, FF_CHUNK * (j + 1))
        gate = jnp.concatenate([gate_halo[:, cols], up[0]], axis=0)
        gate_halo[:, cols] = up[0][tm - HALO:]
        conv = cw_ref[0:1, cols] * gate
        for t in range(1, CONV_W):
            conv = cw_ref[t:t + 1, cols] * gate + pltpu.roll(conv, 1, 0)
        conv = conv[HALO:] + cb_ref[:, cols]
        acts.append((_gelu_sigmoid(conv) * up[1]).astype(BF16))
        if len(acts) == FF_GROUP or j + 1 == n_chunks:
            rows = slice(FF_CHUNK * (j + 1 - len(acts)), FF_CHUNK * (j + 1))
            down = _dot(jnp.concatenate(acts, axis=-1), wdn_ref[rows, :])
            acc = down if acc is None else acc + down
            acts = []
        up = up_next
    x2 = x + mod_ref[0, 5:6, :] * acc
    return _rms(x2, fg_ref[...]) if last else x2


def _mixffn_kernel(y_ref, a_ref, x_ref, mod_ref, wglu_ref, bglu_ref, ag_ref, sg_ref, wout_ref,
                   ln_ref, wup_ref, cw_ref, cb_ref, wdn_ref, fg_ref, o_ref, gate_halo, *, n_chunks, last):
    @pl.when(pl.program_id(1) == 0)
    def _():
        gate_halo[...] = jnp.zeros(gate_halo.shape, F32)

    x1 = _mixer_output(y_ref, a_ref, x_ref, mod_ref, wglu_ref, bglu_ref, ag_ref, sg_ref, wout_ref)
    o_ref[0] = _channel_mixer(x1, gate_halo, mod_ref, ln_ref, wup_ref, cw_ref, cb_ref, wdn_ref, fg_ref,
                              n_chunks=n_chunks, last=last)


def _mixffn(y, a, x, mod, w_glu, b_glu, attn_g, ssm_g, w_out, ln2_g, w_up_r, conv_w, conv_b, w_down, final_g,
            tm, last):
    bsz, seq, d = x.shape
    nslab = y.shape[1]
    d_ssm, d_attn = nslab * y.shape[3], a.shape[-1]
    d_ff = w_down.shape[0]
    n_chunks = d_ff // FF_CHUNK
    const = lambda shape: pl.BlockSpec(shape, lambda b, i: (0,) * len(shape))
    weight = lambda w: pl.BlockSpec(w.shape, lambda b, i: (0,) * w.ndim, pipeline_mode=pl.Buffered(1))
    tile = lambda w: pl.BlockSpec((1, tm, w), lambda b, i: (b, i, 0))
    return pl.pallas_call(
        functools.partial(_mixffn_kernel, n_chunks=n_chunks, last=last),
        grid=(bsz, seq // tm),
        in_specs=[pl.BlockSpec((1, nslab, tm, 128), lambda b, i: (b, 0, i, 0)), tile(d_attn), tile(d),
                  pl.BlockSpec((1, 6, d), lambda b, i: (b, 0, 0)),
                  weight(w_glu), const((1, 2 * d_ssm)), const((1, d_attn)), const((1, d_ssm)), weight(w_out),
                  const((1, d)), weight(w_up_r), const(conv_w.shape), const(conv_b.shape), weight(w_down),
                  const((1, d))],
        out_specs=tile(d),
        out_shape=jax.ShapeDtypeStruct((bsz, seq, d), F32),
        scratch_shapes=[pltpu.VMEM((HALO, d_ff), F32)],
        compiler_params=_params("arbitrary", "arbitrary"),
        name="mixffn",
    )(y, a, x, mod, w_glu, b_glu.reshape(1, -1), attn_g.reshape(1, -1), ssm_g.reshape(1, -1), w_out,
      ln2_g.reshape(1, d), w_up_r, conv_w, conv_b, w_down, final_g.reshape(1, d))


def _layer(x, mod, positions, w_in, ln1_g, q_norm_g, w_uq, kv_norm_g, w_ukv, ssm, w_glu, b_glu,
           attn_out_g, ssm_out_g, w_out, ln2_g, w_up, conv_w, conv_b, w_down, final_g, last):
    bsz, seq, d = x.shape
    tm = min(512, seq)
    half = QK_ROPE // 2
    assert seq % SSM_TOK == 0 and seq % tm == 0 and tm % ATT_BLK == 0 and tm % (HALO * MIX_SPLIT) == 0
    assert d % 128 == 0 and w_down.shape[0] % FF_CHUNK == 0
    o_kr = Q_LORA + KV_LORA
    o_u = o_kr + QK_ROPE
    d_ssm = w_in.shape[1] - o_u
    kr_w = w_in[:, o_kr:o_u]
    w_in_r = jnp.concatenate([w_in[:, :o_kr], w_in[:, o_u:], kr_w, kr_w], axis=1).astype(BF16)
    w_uqt = w_uq.T.astype(BF16)
    ukv = w_ukv.reshape(KV_LORA, N_HEADS, QK_NOPE + V_HEAD)
    w_uk = ukv[:, :, :QK_NOPE].reshape(KV_LORA, -1).astype(BF16)
    w_uvt = ukv[:, :, QK_NOPE:].reshape(KV_LORA, -1).T.astype(BF16)

    qt, k, vt, u = _inproj(x, mod, ln1_g, w_in_r, q_norm_g, w_uqt, kv_norm_g, w_uk, w_uvt, positions, tm)
    a = _attention(qt, k, vt)

    y = _ungroup(_ssm(_regroup(u), *ssm, bsz), bsz)

    return _mixffn(y, a, x, mod, w_glu.astype(BF16), b_glu, attn_out_g, ssm_out_g, w_out.astype(BF16), ln2_g,
                   w_up.astype(BF16), conv_w, conv_b.reshape(1, -1), w_down.astype(BF16), final_g,
                   min(MIX_TILE, seq), last)


def kernel(x, c, positions, w_mod, b_mod, ln1_g, w_in, q_norm_g, w_uq, kv_norm_g, w_ukv, ssm_lam_re, ssm_lam_im, ssm_log_dt, ssm_b_re, ssm_b_im, ssm_c_re, ssm_c_im, ssm_d, w_glu, b_glu, attn_out_g, ssm_out_g, w_out, ln2_g, w_up, conv_w, conv_b, w_down, final_g):
    bsz, seq, d = x.shape
    depth = w_in.shape[0]
    for l in range(depth):
        mod = _mod(c, w_mod[l], b_mod[l]).reshape(bsz, 6, d)
        ssm = _ssm_operators(ssm_lam_re[l], ssm_lam_im[l], ssm_log_dt[l], ssm_b_re[l], ssm_b_im[l],
                             ssm_c_re[l], ssm_c_im[l], ssm_d[l])
        x = _layer(x, mod, positions, w_in[l], ln1_g[l], q_norm_g[l], w_uq[l], kv_norm_g[l], w_ukv[l], ssm,
                   w_glu[l], b_glu[l], attn_out_g[l], ssm_out_g[l], w_out[l], ln2_g[l], w_up[l], conv_w[l],
                   conv_b[l], w_down[l], final_g, l == depth - 1)
    return x
```

```python
import functools
import math

import jax
import jax.numpy as jnp
from jax import lax
from jax.experimental import pallas as pl
from jax.experimental.pallas import tpu as pltpu

N_HEADS = 4
QK_NOPE = 128
QK_ROPE = 64
V_HEAD = 128
Q_LORA = 384
KV_LORA = 256
ROPE_THETA = 10000.0
SSM_GROUP = 16
SSM_STATE = 64
CONV_W = 3
EPS = 1e-6

HEAD_PAD = 256
V_PAD = V_HEAD + 16
ATT_BLK = 256
ATT_CHAINS = 8
ATT_UNROLL = 8
SSM_CHUNK = 16
SSM_TOK = 2048
FF_CHUNK = 256
FF_GROUP = 6
MIX_SPLIT = 2
MIX_TILE = 1024
IN_TILE = 1024
HALO = 8
VMEM_LIMIT = 56 * 1024 * 1024

F32 = jnp.float32
BF16 = jnp.bfloat16


def _rms(x, g):
    return x * lax.rsqrt(jnp.mean(x * x, axis=-1, keepdims=True) + EPS) * g


def _gelu(x):
    return 0.5 * x * (1.0 + jnp.tanh(math.sqrt(2.0 / math.pi) * (x + 0.044715 * (x * x * x))))


def _gelu_sigmoid(x):
    a = -2.0 * math.sqrt(2.0 / math.pi) * math.log2(math.e)
    return x * (1.0 / (1.0 + jnp.exp2(x * (a + (a * 0.044715) * (x * x)))))


def _sigmoid(x):
    return 1.0 / (1.0 + jnp.exp(-x))


def _dot(a, b):
    return jnp.dot(a, b, preferred_element_type=F32)


def _params(*sem, flags=None):
    return pltpu.CompilerParams(dimension_semantics=sem, vmem_limit_bytes=VMEM_LIMIT, flags=flags)


def _mod_kernel(c_ref, w_ref, b_ref, o_ref):
    c = c_ref[...]
    cond = c * _sigmoid(c)
    o_ref[...] = jnp.dot(cond, w_ref[...], preferred_element_type=F32,
                         precision=lax.Precision.HIGHEST) + b_ref[...]


def _mod(c, w_mod, b_mod):
    bsz, d = c.shape
    n = w_mod.shape[1]
    tn = 1024
    return pl.pallas_call(
        _mod_kernel,
        grid=(n // tn,),
        in_specs=[pl.BlockSpec((bsz, d), lambda j: (0, 0)),
                  pl.BlockSpec((d, tn), lambda j: (0, j)),
                  pl.BlockSpec((1, tn), lambda j: (0, j))],
        out_specs=pl.BlockSpec((bsz, tn), lambda j: (0, j)),
        out_shape=jax.ShapeDtypeStruct((bsz, n), F32),
        compiler_params=_params("arbitrary"),
        name="mod",
    )(c, w_mod, b_mod.reshape(1, n))


_NT = (((1,), (1,)), ((), ()))


def _inproj_kernel(x_ref, mod_ref, ln_ref, win_ref, qg_ref, wuqt_ref, kvg_ref, wuk_ref, wuvt_ref,
                   pos_ref, freq_ref, qt_ref, k_ref, vt_ref, u_ref, *, scale, tm):
    x = x_ref[0]
    h = _rms(x, ln_ref[...]) * (1.0 + mod_ref[0, 1:2, :]) + mod_ref[0, 0:1, :]
    z = _dot(h.astype(BF16), win_ref[...])
    o_kv = Q_LORA
    o_u = o_kv + KV_LORA
    o_kr = z.shape[1] - 128
    for slab in range(u_ref.shape[1]):
        u_ref[0, slab] = z[:, o_u + 128 * slab:o_u + 128 * (slab + 1)]
    zqn = _rms(z[:, :o_kv], qg_ref[...]).astype(BF16)
    zkvn = _rms(z[:, o_kv:o_u], kvg_ref[...]).astype(BF16)
    qt = lax.dot_general(wuqt_ref[...], zqn, _NT, preferred_element_type=F32)
    vt = lax.dot_general(wuvt_ref[...], zkvn, _NT, preferred_element_type=F32)
    kn = _dot(zkvn, wuk_ref[...])
    ang = freq_ref[...] * pos_ref[0]
    cos_t = jnp.cos(ang)
    sin_t = jnp.sin(ang)
    half = QK_ROPE // 2
    zero_t = jnp.zeros((128 - QK_ROPE, tm), F32)
    cc = jnp.concatenate([cos_t, cos_t, zero_t], axis=0).T
    ss = jnp.concatenate([-sin_t, sin_t, zero_t], axis=0).T
    zkr = z[:, o_kr:]
    kr = (zkr * cc + pltpu.roll(zkr, half, 1) * ss).astype(BF16)
    ones_rows = (lax.broadcasted_iota(jnp.int32, (V_PAD - V_HEAD, tm), 0) == 0).astype(F32)
    for hd in range(N_HEADS):
        base = (QK_NOPE + QK_ROPE) * hd
        r1 = qt[base + QK_NOPE:base + QK_NOPE + half]
        r2 = qt[base + QK_NOPE + half:base + QK_NOPE + QK_ROPE]
        head = jnp.concatenate([qt[base:base + QK_NOPE], r1 * cos_t - r2 * sin_t, r2 * cos_t + r1 * sin_t,
                                zero_t[:HEAD_PAD - QK_NOPE - QK_ROPE]], axis=0)
        head = (head * scale).astype(BF16)
        vth = jnp.concatenate([vt[V_HEAD * hd:V_HEAD * (hd + 1)], ones_rows], axis=0).astype(BF16)
        for jj in range(tm // ATT_BLK):
            qt_ref[0, hd, jj] = head[:, ATT_BLK * jj:ATT_BLK * (jj + 1)]
            vt_ref[0, hd, jj] = vth[:, ATT_BLK * jj:ATT_BLK * (jj + 1)]
        k_ref[0, hd, :, :QK_NOPE] = kn[:, QK_NOPE * hd:QK_NOPE * (hd + 1)].astype(BF16)
        k_ref[0, hd, :, QK_NOPE:] = kr


def _inproj(x, mod, ln1_g, w_in_r, q_norm_g, w_uqt, kv_norm_g, w_uk, w_uvt, positions, tm):
    bsz, seq, d = x.shape
    d_ssm = w_in_r.shape[1] - Q_LORA - KV_LORA - 128
    half = QK_ROPE // 2
    inv_freq = ROPE_THETA ** (-jnp.arange(0, QK_ROPE, 2, dtype=F32) / QK_ROPE)
    nb = tm // ATT_BLK
    const = lambda shape: pl.BlockSpec(shape, lambda b, i: (0,) * len(shape))
    scale = (QK_NOPE + QK_ROPE) ** -0.5 * math.log2(math.e)
    return pl.pallas_call(
        functools.partial(_inproj_kernel, scale=scale, tm=tm),
        grid=(bsz, seq // tm),
        in_specs=[pl.BlockSpec((1, tm, d), lambda b, i: (b, i, 0)),
                  pl.BlockSpec((1, 6, d), lambda b, i: (b, 0, 0)),
                  const((1, d)), const(w_in_r.shape), const((1, Q_LORA)), const(w_uqt.shape),
                  const((1, KV_LORA)), const(w_uk.shape), const(w_uvt.shape),
                  pl.BlockSpec((1, 1, tm), lambda b, i: (b, 0, i)), const((half, 1))],
        out_specs=[pl.BlockSpec((1, N_HEADS, nb, HEAD_PAD, ATT_BLK), lambda b, i: (b, 0, i, 0, 0)),
                   pl.BlockSpec((1, N_HEADS, tm, HEAD_PAD), lambda b, i: (b, 0, i, 0)),
                   pl.BlockSpec((1, N_HEADS, nb, V_PAD, ATT_BLK), lambda b, i: (b, 0, i, 0, 0)),
                   pl.BlockSpec((1, d_ssm // 128, tm, 128), lambda b, i: (b, 0, i, 0))],
        out_shape=[jax.ShapeDtypeStruct((bsz, N_HEADS, seq // ATT_BLK, HEAD_PAD, ATT_BLK), BF16),
                   jax.ShapeDtypeStruct((bsz, N_HEADS, seq, HEAD_PAD), BF16),
                   jax.ShapeDtypeStruct((bsz, N_HEADS, seq // ATT_BLK, V_PAD, ATT_BLK), BF16),
                   jax.ShapeDtypeStruct((bsz, d_ssm // 128, seq, 128), F32)],
        compiler_params=_params("arbitrary", "arbitrary"),
        name="inproj",
    )(x, mod, ln1_g.reshape(1, d), w_in_r, q_norm_g.reshape(1, -1), w_uqt,
      kv_norm_g.reshape(1, -1), w_uk, w_uvt, positions.astype(F32).reshape(bsz, 1, seq), inv_freq.reshape(half, 1))


def _attn_kernel(qt_ref, k_ref, vt_ref, o_ref, m_sc, acc_sc, s_sc, *, blk, nq, dv):
    qi = pl.program_id(2)
    m_sc[...] = jnp.full(m_sc.shape, -jnp.inf, F32)
    acc_sc[...] = jnp.zeros(acc_sc.shape, F32)

    def key_rows(j):
        return k_ref[0, 0, pl.ds(pl.multiple_of(j * blk, blk), blk), :]

    def scores(j, chains):
        k = key_rows(j)
        return [_dot(k, qt_ref[0, 0, c]) for c in chains]

    def softmax_values(items):
        probs = []
        for c, j, s, masked in items:
            if masked:
                key = lax.broadcasted_iota(jnp.int32, s.shape, 0)
                qry = lax.broadcasted_iota(jnp.int32, s.shape, 1)
                s = jnp.where(key <= qry, s, -1e30)
            m = m_sc[c]
            m_new = jnp.maximum(m, jnp.max(s, axis=0, keepdims=True))
            alpha = jnp.exp2(m - m_new)
            p = jnp.exp2(s - m_new)
            m_sc[c] = m_new
            probs.append((alpha, p.astype(BF16)))
        for (c, j, _, _), (alpha, p) in zip(items, probs):
            acc_sc[c] = alpha * acc_sc[c] + _dot(vt_ref[0, 0, j], p)

    every = range(nq)
    nfull = qi * nq

    def step(j, slot):
        for c, s in zip(every, scores(j + 1, every)):
            s_sc[1 - slot, c] = s
        softmax_values([(c, j, s_sc[slot, c], False) for c in every])

    def body(i, carry):
        for t in range(ATT_UNROLL):
            step(ATT_UNROLL * i + t, t % 2)
        return carry

    for c, s in zip(every, scores(0, every)):
        s_sc[0, c] = s
    lax.fori_loop(0, nfull // ATT_UNROLL, body, 0)

    first = lambda c: 0 if c >= nq // 2 else nq - 1 - c
    rounds = [[(c, r - first(c)) for c in every if 0 <= r - first(c) <= c] for r in range(nq)]

    def issue(pairs):
        keys = {kc: key_rows(nfull + kc) for kc in sorted({kc for _, kc in pairs if kc > 0})}
        return [_dot(keys[kc], qt_ref[0, 0, c]) if kc > 0 else None for c, kc in pairs]

    s_cur = issue(rounds[0])
    for r in range(nq):
        s_next = issue(rounds[r + 1]) if r + 1 < nq else None
        softmax_values([(c, nfull + kc, s_sc[0, c] if s is None else s, kc == c)
                        for (c, kc), s in zip(rounds[r], s_cur)])
        s_cur = s_next
    for c in range(nq):
        acc = acc_sc[c]
        o_ref[0, blk * c:blk * (c + 1), :] = (acc[:dv] / acc[dv:dv + 1]).T.astype(o_ref.dtype)


def _attention(qt, k, vt):
    bsz, nh, nblk, dh, blk = qt.shape
    dvp = vt.shape[3]
    dv = V_HEAD
    seq = nblk * blk
    nq = min(ATT_CHAINS, nblk)
    assert nq % ATT_UNROLL == 0 and ATT_UNROLL % 2 == 0 and nblk % nq == 0
    return pl.pallas_call(
        functools.partial(_attn_kernel, blk=blk, nq=nq, dv=dv),
        grid=(bsz, nh, nblk // nq),
        in_specs=[pl.BlockSpec((1, 1, nq, dh, blk), lambda b, h, i: (b, h, i, 0, 0)),
                  pl.BlockSpec((1, 1, seq, dh), lambda b, h, i: (b, h, 0, 0)),
                  pl.BlockSpec((1, 1, nblk, dvp, blk), lambda b, h, i: (b, h, 0, 0, 0))],
        out_specs=pl.BlockSpec((1, nq * blk, dv), lambda b, h, i: (b, i, h)),
        out_shape=jax.ShapeDtypeStruct((bsz, seq, nh * dv), BF16),
        scratch_shapes=[pltpu.VMEM((nq, 1, blk), F32), pltpu.VMEM((nq, dvp, blk), F32),
                        pltpu.VMEM((2, nq, blk, blk), F32)],
        compiler_params=_params("arbitrary", "arbitrary", "arbitrary"),
        name="attention",
    )(qt, k, vt)


def _slab_rows(slab, g8, s):
    g = 8 * slab + g8
    return g // 2, (g % 2) * SSM_CHUNK * SSM_GROUP + SSM_GROUP * s


def _regroup_kernel(u_ref, z_ref):
    nslab = u_ref.shape[1]
    nc = u_ref.shape[2] // SSM_CHUNK
    for slab in range(nslab):
        for s in range(SSM_CHUNK):
            t = u_ref[0, slab, pl.ds(s, nc, stride=SSM_CHUNK), :].T.astype(BF16)
            for g8 in range(128 // SSM_GROUP):
                pair, row = _slab_rows(slab, g8, s)
                z_ref[pair, row:row + SSM_GROUP, :] = t[SSM_GROUP * g8:SSM_GROUP * (g8 + 1), :]


def _regroup(u):
    bsz, nslab, seq, _ = u.shape
    nck = seq // SSM_CHUNK
    npair = nslab * 128 // SSM_GROUP // 2
    width = 2 * SSM_CHUNK * SSM_GROUP
    nt = seq // SSM_TOK
    return pl.pallas_call(
        _regroup_kernel,
        grid=(bsz, nt),
        in_specs=[pl.BlockSpec((1, nslab, SSM_TOK, 128), lambda b, i: (b, 0, i, 0))],
        out_specs=pl.BlockSpec((npair, width, SSM_TOK // SSM_CHUNK), lambda b, i: (0, 0, b * nt + i)),
        out_shape=jax.ShapeDtypeStruct((npair, width, bsz * nck), BF16),
        compiler_params=_params("arbitrary", "arbitrary"),
        name="regroup",
    )(u)


def _ungroup_kernel(yt_ref, y_ref):
    nslab = y_ref.shape[1]
    nc = y_ref.shape[2] // SSM_CHUNK
    for slab in range(nslab):
        for s in range(SSM_CHUNK):
            pieces = []
            for g8 in range(128 // SSM_GROUP):
                pair, row = _slab_rows(slab, g8, s)
                pieces.append(yt_ref[pair, row:row + SSM_GROUP, :])
            y_ref[0, slab, pl.ds(s, nc, stride=SSM_CHUNK), :] = jnp.concatenate(pieces, axis=0).astype(F32).T


def _ungroup(yt, bsz):
    npair, width, ncol = yt.shape
    seq = ncol // bsz * SSM_CHUNK
    nslab = npair * 2 * SSM_GROUP // 128
    nt = seq // SSM_TOK
    return pl.pallas_call(
        _ungroup_kernel,
        grid=(bsz, nt),
        in_specs=[pl.BlockSpec((npair, width, SSM_TOK // SSM_CHUNK), lambda b, i: (0, 0, b * nt + i))],
        out_specs=pl.BlockSpec((1, nslab, SSM_TOK, 128), lambda b, i: (b, 0, i, 0)),
        out_shape=jax.ShapeDtypeStruct((bsz, nslab, seq, 128), F32),
        compiler_params=_params("arbitrary", "arbitrary"),
        name="ungroup",
    )(yt)


def _ssm_kernel(z_ref, toep_ref, r_ref, o_ref, are_ref, aim_ref, y_ref, rt_sc, xp_sc, *, bsz, nck):
    ncol = bsz * nck
    cb = min(512, ncol)
    ns = are_ref.shape[-1]
    gw = z_ref.shape[1] // 2
    zg = lambda g2, c0: z_ref[0, gw * g2:gw * (g2 + 1), c0:c0 + cb]
    for c0 in range(0, ncol, cb):
        r0 = _dot(r_ref[0], zg(0, c0))
        r1 = _dot(r_ref[1], zg(1, c0))
        r_re = jnp.concatenate([r0[:ns // 2], r1[:ns // 2]], axis=0)
        r_im = jnp.concatenate([r0[ns // 2:], r1[ns // 2:]], axis=0)
        for q in range(0, cb, 128):
            rt_sc[0, c0 + q:c0 + q + 128, :] = r_re[:, q:q + 128].T
            rt_sc[1, c0 + q:c0 + q + 128, :] = r_im[:, q:q + 128].T
    a_re = jnp.broadcast_to(are_ref[0], (bsz, ns))
    a_im = jnp.broadcast_to(aim_ref[0], (bsz, ns))

    def step(c, carry):
        s_re, s_im = carry
        rows = pl.ds(c, bsz, stride=nck)
        xp_sc[0, rows, :] = s_re
        xp_sc[1, rows, :] = s_im
        n_re = a_re * s_re - a_im * s_im + rt_sc[0, rows, :]
        n_im = a_re * s_im + a_im * s_re + rt_sc[1, rows, :]
        return n_re, n_im

    zero = jnp.zeros((bsz, ns), F32)
    lax.fori_loop(0, nck, step, (zero, zero), unroll=4)
    for c0 in range(0, ncol, cb):
        xp = jnp.concatenate([xp_sc[0, c0:c0 + cb, :], xp_sc[1, c0:c0 + cb, :]], axis=-1).astype(BF16)
        y = lax.dot_general(o_ref[0], xp, _NT, preferred_element_type=F32)
        y += jnp.concatenate([_dot(toep_ref[0], zg(0, c0)), _dot(toep_ref[1], zg(1, c0))], axis=0)
        y_ref[0, :, c0:c0 + cb] = y.astype(y_ref.dtype)


def _ssm(z, toep, r, o, a_re, a_im, bsz):
    npair, width, ncol = z.shape
    ns = a_re.shape[-1]
    blk = lambda a: pl.BlockSpec((1,) + a.shape[1:], lambda g: (g,) + (0,) * (a.ndim - 1))
    two = lambda a: pl.BlockSpec((2,) + a.shape[1:], lambda g: (g,) + (0,) * (a.ndim - 1))
    return pl.pallas_call(
        functools.partial(_ssm_kernel, bsz=bsz, nck=ncol // bsz),
        grid=(npair,),
        in_specs=[blk(z), two(toep), two(r), blk(o), blk(a_re), blk(a_im)],
        out_specs=pl.BlockSpec((1, width, ncol), lambda g: (g, 0, 0)),
        out_shape=jax.ShapeDtypeStruct((npair, width, ncol), BF16),
        scratch_shapes=[pltpu.VMEM((2, ncol, ns), F32)] * 2,
        compiler_params=_params("arbitrary"),
        name="ssm",
    )(z, toep, r, o, a_re, a_im)


def _ssm_operators(lam_re, lam_im, log_dt, b_re, b_im, c_re, c_im, d_skip):
    t_len = SSM_CHUNK
    g, p = lam_re.shape
    hch = b_re.shape[-1]
    lr = jnp.minimum(lam_re.astype(F32), -1e-4)
    li = lam_im.astype(F32)
    dt = jnp.exp(log_dt.astype(F32))[:, None]
    mag = jnp.exp(lr * dt)
    ab_re = mag * jnp.cos(li * dt)
    ab_im = mag * jnp.sin(li * dt)
    den = lr * lr + li * li
    nr, ni = ab_re - 1.0, ab_im
    z_re = ((nr * lr + ni * li) / den)[..., None]
    z_im = ((ni * lr - nr * li) / den)[..., None]
    br, bi = b_re.astype(F32), b_im.astype(F32)
    bb_re = z_re * br - z_im * bi
    bb_im = z_re * bi + z_im * br
    tau = jnp.arange(t_len + 1, dtype=F32)[None, :, None]
    pm = jnp.exp(tau * (lr * dt)[:, None, :])
    pw_re = pm * jnp.cos(tau * (li * dt)[:, None, :])
    pw_im = pm * jnp.sin(tau * (li * dt)[:, None, :])
    cr, ci = c_re.astype(F32)[:, None], c_im.astype(F32)[:, None]
    ca_re = cr * pw_re[:, :, None, :] - ci * pw_im[:, :, None, :]
    ca_im = cr * pw_im[:, :, None, :] + ci * pw_re[:, :, None, :]
    ca = jnp.concatenate([ca_re[:, :t_len], -ca_im[:, :t_len]], axis=-1).transpose(0, 2, 1, 3)
    bb = jnp.concatenate([bb_re, bb_im], axis=1).transpose(0, 2, 1)
    kern = jnp.sum(ca[:, :, :, None, :] * bb[:, None, None, :, :], axis=-1)
    kern = kern.at[:, :, 0, :].add(jnp.eye(hch, dtype=F32)[None] * d_skip.astype(F32)[:, :, None])
    krow = jnp.concatenate([kern[:, :, ::-1, :].reshape(g, hch, t_len * hch),
                            jnp.zeros((g, hch, (t_len - 1) * hch), F32)], axis=-1)
    toep = jnp.stack([krow[:, :, (t_len - 1 - t) * hch:(2 * t_len - 1 - t) * hch] for t in range(t_len)], axis=1)
    toep = toep.reshape(g, t_len * hch, t_len * hch)
    rp_re = pw_re[:, :t_len][:, ::-1].transpose(0, 2, 1)[..., None]
    rp_im = pw_im[:, :t_len][:, ::-1].transpose(0, 2, 1)[..., None]
    rr = (rp_re * bb_re[:, :, None, :] - rp_im * bb_im[:, :, None, :]).reshape(g, p, t_len * hch)
    ri = (rp_re * bb_im[:, :, None, :] + rp_im * bb_re[:, :, None, :]).reshape(g, p, t_len * hch)
    orr = ca_re[:, 1:].reshape(g, t_len * hch, p)
    oii = (-ca_im[:, 1:]).reshape(g, t_len * hch, p)

    def pair_diag(m):
        m = m.reshape(g // 2, 2, m.shape[1], m.shape[2])
        z = jnp.zeros_like(m[:, 0])
        top = jnp.concatenate([m[:, 0], z], axis=2)
        bot = jnp.concatenate([z, m[:, 1]], axis=2)
        return jnp.concatenate([top, bot], axis=1).astype(BF16)

    a_t_re = pw_re[:, t_len].reshape(g // 2, 1, 2 * p)
    a_t_im = pw_im[:, t_len].reshape(g // 2, 1, 2 * p)
    r_g = jnp.concatenate([rr, ri], axis=1).astype(BF16)
    o_pair = jnp.concatenate([pair_diag(orr), pair_diag(oii)], axis=2)
    return toep.astype(BF16), r_g, o_pair, a_t_re, a_t_im


def _mixer_output(y_ref, a_ref, x_ref, mod_ref, wglu_ref, bglu_ref, ag_ref, sg_ref, wout_ref):
    d_ssm = y_ref.shape[1] * y_ref.shape[3]
    d_attn = a_ref.shape[-1]
    tm = x_ref.shape[1]
    parts = [slice(r, r + tm // MIX_SPLIT) for r in range(0, tm, tm // MIX_SPLIT)]
    gls = []
    for rows in parts:
        y = jnp.concatenate([y_ref[0, slab, rows, :] for slab in range(y_ref.shape[1])], axis=-1)
        gls.append(_dot(_gelu_sigmoid(y).astype(BF16), wglu_ref[...]) + bglu_ref[...])
    outs = []
    for rows, gl in zip(parts, gls):
        s = gl[:, :d_ssm] * _sigmoid(gl[:, d_ssm:])
        sn = _rms(s, sg_ref[...]).astype(BF16)
        an = _rms(a_ref[0, rows, :].astype(F32), ag_ref[...]).astype(BF16)
        m = _dot(an, wout_ref[:d_attn, :]) + _dot(sn, wout_ref[d_attn:, :])
        outs.append(x_ref[0, rows, :] + mod_ref[0, 2:3, :] * m)
    return jnp.concatenate(outs, axis=0)


def _channel_mixer(x, gate_halo, mod_ref, ln_ref, wup_ref, cw_ref, cb_ref, wdn_ref, fg_ref, *, n_chunks, last):
    tm = x.shape[0]
    d_ff = wdn_ref.shape[0]
    h = (_rms(x, ln_ref[...]) * (1.0 + mod_ref[0, 4:5, :]) + mod_ref[0, 3:4, :]).astype(BF16)

    def up_proj(j):
        cols = slice(FF_CHUNK * j, FF_CHUNK * (j + 1))
        vals = slice(d_ff + FF_CHUNK * j, d_ff + FF_CHUNK * (j + 1))
        return _dot(h, wup_ref[:, cols]), _dot(h, wup_ref[:, vals])

    up = up_proj(0)
    acc = None
    acts = []
    for j in range(n_chunks):
        up_next = up_proj(j + 1) if j + 1 < n_chunks else None
        cols = slice(FF_CHUNK * j, FF_CHUNK * (j + 1))
        gate = jnp.concatenate([gate_halo[:, cols], up[0]], axis=0)
        gate_halo[:, cols] = up[0][tm - HALO:]
        conv = cw_ref[0:1, cols] * gate
        for t in range(1, CONV_W):
            conv = cw_ref[t:t + 1, cols] * gate + pltpu.roll(conv, 1, 0)
        conv = conv[HALO:] + cb_ref[:, cols]
        acts.append((_gelu_sigmoid(conv) * up[1]).astype(BF16))
        if len(acts) == FF_GROUP or j + 1 == n_chunks:
            rows = slice(FF_CHUNK * (j + 1 - len(acts)), FF_CHUNK * (j + 1))
            down = _dot(jnp.concatenate(acts, axis=-1), wdn_ref[rows, :])
            acc = down if acc is None else acc + down
            acts = []
        up = up_next
    x2 = x + mod_ref[0, 5:6, :] * acc
    return _rms(x2, fg_ref[...]) if last else x2


def _mixffn_kernel(y_ref, a_ref, x_ref, mod_ref, wglu_ref, bglu_ref, ag_ref, sg_ref, wout_ref,
                   ln_ref, wup_ref, cw_ref, cb_ref, wdn_ref, fg_ref, o_ref, gate_halo, *, n_chunks, last):
    @pl.when(pl.program_id(1) == 0)
    def _():
        gate_halo[...] = jnp.zeros(gate_halo.shape, F32)

    x1 = _mixer_output(y_ref, a_ref, x_ref, mod_ref, wglu_ref, bglu_ref, ag_ref, sg_ref, wout_ref)
    o_ref[0] = _channel_mixer(x1, gate_halo, mod_ref, ln_ref, wup_ref, cw_ref, cb_ref, wdn_ref, fg_ref,
                              n_chunks=n_chunks, last=last)


def _mixffn(y, a, x, mod, w_glu, b_glu, attn_g, ssm_g, w_out, ln2_g, w_up_r, conv_w, conv_b, w_down, final_g,
            tm, last):
    bsz, seq, d = x.shape
    nslab = y.shape[1]
    d_ssm, d_attn = nslab * y.shape[3], a.shape[-1]
    d_ff = w_down.shape[0]
    n_chunks = d_ff // FF_CHUNK
    const = lambda shape: pl.BlockSpec(shape, lambda b, i: (0,) * len(shape))
    weight = lambda w: pl.BlockSpec(w.shape, lambda b, i: (0,) * w.ndim, pipeline_mode=pl.Buffered(1))
    tile = lambda w: pl.BlockSpec((1, tm, w), lambda b, i: (b, i, 0))
    return pl.pallas_call(
        functools.partial(_mixffn_kernel, n_chunks=n_chunks, last=last),
        grid=(bsz, seq // tm),
        in_specs=[pl.BlockSpec((1, nslab, tm, 128), lambda b, i: (b, 0, i, 0)), tile(d_attn), tile(d),
                  pl.BlockSpec((1, 6, d), lambda b, i: (b, 0, 0)),
                  weight(w_glu), const((1, 2 * d_ssm)), const((1, d_attn)), const((1, d_ssm)), weight(w_out),
                  const((1, d)), weight(w_up_r), const(conv_w.shape), const(conv_b.shape), weight(w_down),
                  const((1, d))],
        out_specs=tile(d),
        out_shape=jax.ShapeDtypeStruct((bsz, seq, d), F32),
        scratch_shapes=[pltpu.VMEM((HALO, d_ff), F32)],
        compiler_params=_params("arbitrary", "arbitrary"),
        name="mixffn",
    )(y, a, x, mod, w_glu, b_glu.reshape(1, -1), attn_g.reshape(1, -1), ssm_g.reshape(1, -1), w_out,
      ln2_g.reshape(1, d), w_up_r, conv_w, conv_b, w_down, final_g.reshape(1, d))


def _layer(x, mod, positions, w_in, ln1_g, q_norm_g, w_uq, kv_norm_g, w_ukv, ssm, w_glu, b_glu,
           attn_out_g, ssm_out_g, w_out, ln2_g, w_up, conv_w, conv_b, w_down, final_g, last):
    bsz, seq, d = x.shape
    tm = min(IN_TILE, seq)
    half = QK_ROPE // 2
    assert seq % SSM_TOK == 0 and seq % tm == 0 and tm % ATT_BLK == 0 and tm % (HALO * MIX_SPLIT) == 0
    assert d % 128 == 0 and w_down.shape[0] % FF_CHUNK == 0
    o_kr = Q_LORA + KV_LORA
    o_u = o_kr + QK_ROPE
    d_ssm = w_in.shape[1] - o_u
    kr_w = w_in[:, o_kr:o_u]
    w_in_r = jnp.concatenate([w_in[:, :o_kr], w_in[:, o_u:], kr_w, kr_w], axis=1).astype(BF16)
    w_uqt = w_uq.T.astype(BF16)
    ukv = w_ukv.reshape(KV_LORA, N_HEADS, QK_NOPE + V_HEAD)
    w_uk = ukv[:, :, :QK_NOPE].reshape(KV_LORA, -1).astype(BF16)
    w_uvt = ukv[:, :, QK_NOPE:].reshape(KV_LORA, -1).T.astype(BF16)

    qt, k, vt, u = _inproj(x, mod, ln1_g, w_in_r, q_norm_g, w_uqt, kv_norm_g, w_uk, w_uvt, positions, tm)
    a = _attention(qt, k, vt)

    y = _ungroup(_ssm(_regroup(u), *ssm, bsz), bsz)

    return _mixffn(y, a, x, mod, w_glu.astype(BF16), b_glu, attn_out_g, ssm_out_g, w_out.astype(BF16), ln2_g,
                   w_up.astype(BF16), conv_w, conv_b.reshape(1, -1), w_down.astype(BF16), final_g,
                   min(MIX_TILE, seq), last)


def kernel(x, c, positions, w_mod, b_mod, ln1_g, w_in, q_norm_g, w_uq, kv_norm_g, w_ukv, ssm_lam_re, ssm_lam_im, ssm_log_dt, ssm_b_re, ssm_b_im, ssm_c_re, ssm_c_im, ssm_d, w_glu, b_glu, attn_out_g, ssm_out_g, w_out, ln2_g, w_up, conv_w, conv_b, w_down, final_g):
    bsz, seq, d = x.shape
    depth = w_in.shape[0]
    for l in range(depth):
        mod = _mod(c, w_mod[l], b_mod[l]).reshape(bsz, 6, d)
        ssm = _ssm_operators(ssm_lam_re[l], ssm_lam_im[l], ssm_log_dt[l], ssm_b_re[l], ssm_b_im[l],
                             ssm_c_re[l], ssm_c_im[l], ssm_d[l])
        x = _layer(x, mod, positions, w_in[l], ln1_g[l], q_norm_g[l], w_uq[l], kv_norm_g[l], w_ukv[l], ssm,
                   w_glu[l], b_glu[l], attn_out_g[l], ssm_out_g[l], w_out[l], ln2_g[l], w_up[l], conv_w[l],
                   conv_b[l], w_down[l], final_g, l == depth - 1)
    return x
```
